```python
import math
import jax, jax.numpy as jnp
from jax import lax
import numpy as np

D_MODEL = 4096
BATCH = 4
SEQ = 2048
DEPTH = 1
DEC_BATCH = 32
DEC_SEQ = 8
PAST_LEN = 8192
PAGE_SIZE = 128

HEAD_DIM = 128
N_MIX_SLOTS = D_MODEL // HEAD_DIM
FOX_HEADS = N_MIX_SLOTS // 2
DIFF_HEADS = N_MIX_SLOTS // 4
FOX_WIDTH = FOX_HEADS * HEAD_DIM
DIFF_WIDTH = DIFF_HEADS * 2 * HEAD_DIM
MIX_WIDTH = FOX_WIDTH + DIFF_WIDTH
OFF_FQ = 0
OFF_FK = OFF_FQ + FOX_WIDTH
OFF_FV = OFF_FK + FOX_WIDTH
OFF_FF = OFF_FV + FOX_WIDTH
OFF_DQ = OFF_FF + FOX_HEADS
OFF_DK = OFF_DQ + DIFF_WIDTH
OFF_DV = OFF_DK + DIFF_WIDTH
D_IN = OFF_DV + DIFF_WIDTH
Q_BLOCK = 128
ROPE_THETA = 10000.0
ATTN_SCALE = HEAD_DIM ** -0.5
NEG_INF = -1e30
NORM_EPS = 1e-6
SUBLN_EPS = 1e-5
N_KEYS = 128
N_EXPERTS = N_KEYS * N_KEYS
PEER_HEADS = 8
PEER_TOPK = 16
PEER_QDIM = 256
PEER_HALF = PEER_QDIM // 2
PEER_BLOCK = 128

kernel_name = 'hybrid_fox_diff_peer_step'


def rms_norm(x, g, eps=NORM_EPS):
    xf = x.astype(jnp.float32)
    y = xf * lax.rsqrt(jnp.mean(xf * xf, axis=-1, keepdims=True) + eps)
    return (y * g.astype(jnp.float32)).astype(x.dtype)


def rope(x, pos):
    half = HEAD_DIM // 2
    inv = ROPE_THETA ** (-jnp.arange(0, HEAD_DIM, 2, dtype=jnp.float32) / HEAD_DIM)
    ang = pos.astype(jnp.float32)[:, None] * inv[None, :]
    bshape = (pos.shape[0],) + (1,) * (x.ndim - 3) + (half,)
    cos = jnp.cos(ang).reshape(bshape)
    sin = jnp.sin(ang).reshape(bshape)
    xf = x.astype(jnp.float32)
    x1, x2 = xf[..., :half], xf[..., half:]
    return jnp.concatenate([x1 * cos - x2 * sin, x2 * cos + x1 * sin], axis=-1).astype(x.dtype)


def causal(s, qpos, kpos):
    return jnp.where(kpos[None, :] <= qpos[:, None], s, NEG_INF)


def mixer_inputs(h, w_in, b_forget, g_qf, g_kf, g_qd, g_kd, pos):
    b, t, _ = h.shape
    p = h @ w_in
    qf = rms_norm(p[..., OFF_FQ:OFF_FK].reshape(b, t, FOX_HEADS, HEAD_DIM), g_qf)
    kf = rms_norm(p[..., OFF_FK:OFF_FV].reshape(b, t, FOX_HEADS, HEAD_DIM), g_kf)
    vf = p[..., OFF_FV:OFF_FF].reshape(b, t, FOX_HEADS, HEAD_DIM)
    logf = jax.nn.log_sigmoid(p[..., OFF_FF:OFF_DQ].astype(jnp.float32) + b_forget.astype(jnp.float32))
    qd = rope(rms_norm(p[..., OFF_DQ:OFF_DK].reshape(b, t, DIFF_HEADS, 2, HEAD_DIM), g_qd), pos)
    kd = rope(rms_norm(p[..., OFF_DK:OFF_DV].reshape(b, t, DIFF_HEADS, 2, HEAD_DIM), g_kd), pos)
    vd = p[..., OFF_DV:].reshape(b, t, DIFF_HEADS, 2 * HEAD_DIM)
    return qf, kf, vf, logf, qd, kd, vd


def fox_attend(q, cq, qpos, segs):
    cq_t = jnp.swapaxes(cq, 1, 2)[..., :, None]
    logits = []
    for k, _, ck, kpos in segs:
        s = jnp.einsum('bqhd,bkhd->bhqk', q, k).astype(jnp.float32) * ATTN_SCALE
        s = s + cq_t - jnp.swapaxes(ck, 1, 2)[..., None, :]
        logits.append(causal(s, qpos, kpos))
    p = jax.nn.softmax(jnp.concatenate(logits, axis=-1), axis=-1)
    out, off = 0.0, 0
    for k, v, _, _ in segs:
        n = k.shape[1]
        out = out + jnp.einsum('bhqk,bkhd->bqhd', p[..., off:off + n].astype(v.dtype), v)
        off += n
    return out


def diff_attend(q, lam, qpos, segs):
    logits = []
    for k, _, kpos in segs:
        s = jnp.einsum('bqhcd,bkhcd->bchqk', q, k).astype(jnp.float32) * ATTN_SCALE
        logits.append(causal(s, qpos, kpos))
    p = jax.nn.softmax(jnp.concatenate(logits, axis=-1), axis=-1)
    pd = p[:, 0] - lam * p[:, 1]
    out, off = 0.0, 0
    for k, v, _ in segs:
        n = k.shape[1]
        out = out + jnp.einsum('bhqk,bkhe->bqhe', pd[..., off:off + n].astype(v.dtype), v)
        off += n
    return out


def merge_out(of, od, g_sub, lam_init, w_out):
    b, t = of.shape[0], of.shape[1]
    od = rms_norm(od, g_sub, SUBLN_EPS) * (1.0 - lam_init)
    o = jnp.concatenate([of.reshape(b, t, FOX_WIDTH), od.reshape(b, t, DIFF_WIDTH)], axis=-1)
    return o @ w_out


def peer_ffn(h, w_q, keys_a, keys_b, u, v):
    shp = h.shape
    t = h.reshape(-1, D_MODEL)
    n = t.shape[0]
    n_pad = (-n) % PEER_BLOCK
    blocks = jnp.pad(t, ((0, n_pad), (0, 0))).reshape(-1, PEER_BLOCK, D_MODEL)

    def one_block(tb):
        q = (tb @ w_q).reshape(PEER_BLOCK, PEER_HEADS, 2, PEER_HALF)
        sa = jnp.einsum('thd,kd->thk', q[:, :, 0], keys_a).astype(jnp.float32)
        sb = jnp.einsum('thd,kd->thk', q[:, :, 1], keys_b).astype(jnp.float32)
        va, ia = lax.top_k(sa, PEER_TOPK)
        vb, ib = lax.top_k(sb, PEER_TOPK)
        cand = (va[..., :, None] + vb[..., None, :]).reshape(PEER_BLOCK, PEER_HEADS, PEER_TOPK * PEER_TOPK)
        cidx = (ia[..., :, None] * N_KEYS + ib[..., None, :]).reshape(PEER_BLOCK, PEER_HEADS, PEER_TOPK * PEER_TOPK)
        sv, si = lax.top_k(cand, PEER_TOPK)
        eidx = jnp.take_along_axis(cidx, si, axis=-1)
        g = jax.nn.softmax(sv, axis=-1)
        act = jax.nn.gelu(jnp.einsum('td,thkd->thk', tb, u[eidx]).astype(jnp.float32), approximate=False)
        return jnp.einsum('thk,thkd->td', (g * act).astype(tb.dtype), v[eidx])

    out = lax.map(one_block, blocks).reshape(-1, D_MODEL)[:n]
    return out.reshape(shp)


def gather_pages(pool, l, page_table):
    g = pool[l, page_table]
    return g.reshape((page_table.shape[0], page_table.shape[1] * PAGE_SIZE) + g.shape[3:])


def setup_inputs(seed: int = 0) -> dict:
    key = jax.random.key(seed)
    ks = jax.random.split(key, 32)
    f32 = jnp.float32
    n_pages = PAST_LEN // PAGE_SIZE
    n_used = DEC_BATCH * n_pages
    n_pool = n_used + (n_used + 3) // 4

    def nrm(k, shape, scale):
        return jax.random.normal(k, shape, f32) * scale

    def gain(k, n):
        return 1.0 + 0.02 * jax.random.normal(k, (DEPTH, n), f32)

    x_prompt = nrm(ks[0], (BATCH, SEQ, D_MODEL), 1.0)
    x_sample = nrm(ks[1], (DEC_BATCH, DEC_SEQ, D_MODEL), 1.0)
    cache_fox_k = nrm(ks[2], (DEPTH, n_pool, PAGE_SIZE, FOX_HEADS, HEAD_DIM), 1.0)
    cache_fox_v = nrm(ks[3], (DEPTH, n_pool, PAGE_SIZE, FOX_HEADS, HEAD_DIM), 1.0)
    lf_shape = (DEPTH, n_pool, PAGE_SIZE, FOX_HEADS)
    cache_fox_logf = jax.nn.log_sigmoid(jax.random.uniform(ks[4], lf_shape, f32, 1.0, 6.0) + nrm(ks[5], lf_shape, 1.0))
    cache_diff_k = nrm(ks[6], (DEPTH, n_pool, PAGE_SIZE, DIFF_HEADS, 2, HEAD_DIM), 1.0)
    cache_diff_v = nrm(ks[7], (DEPTH, n_pool, PAGE_SIZE, DIFF_HEADS, 2 * HEAD_DIM), 1.0)
    page_table = jax.random.permutation(ks[8], n_pool)[:n_used].reshape(DEC_BATCH, n_pages).astype(jnp.int32)
    return {
        'x_prompt': x_prompt,
        'x_sample': x_sample,
        'cache_fox_k': cache_fox_k,
        'cache_fox_v': cache_fox_v,
        'cache_fox_logf': cache_fox_logf,
        'cache_diff_k': cache_diff_k,
        'cache_diff_v': cache_diff_v,
        'page_table': page_table,
        'g_attn_norm': gain(ks[9], D_MODEL),
        'w_in': nrm(ks[10], (DEPTH, D_MODEL, D_IN), D_MODEL ** -0.5),
        'b_forget': jax.random.uniform(ks[11], (DEPTH, FOX_HEADS), f32, 1.0, 6.0),
        'g_q_fox': gain(ks[12], HEAD_DIM),
        'g_k_fox': gain(ks[13], HEAD_DIM),
        'g_q_diff': gain(ks[14], HEAD_DIM),
        'g_k_diff': gain(ks[15], HEAD_DIM),
        'lambda_q1': nrm(ks[16], (DEPTH, HEAD_DIM), 0.1),
        'lambda_k1': nrm(ks[17], (DEPTH, HEAD_DIM), 0.1),
        'lambda_q2': nrm(ks[18], (DEPTH, HEAD_DIM), 0.1),
        'lambda_k2': nrm(ks[19], (DEPTH, HEAD_DIM), 0.1),
        'g_diff_sub': gain(ks[20], 2 * HEAD_DIM),
        'w_out': nrm(ks[21], (DEPTH, MIX_WIDTH, D_MODEL), MIX_WIDTH ** -0.5),
        'g_ffn_norm': gain(ks[22], D_MODEL),
        'w_peer_q': nrm(ks[23], (DEPTH, D_MODEL, PEER_HEADS * PEER_QDIM), D_MODEL ** -0.5),
        'peer_keys_a': nrm(ks[24], (DEPTH, N_KEYS, PEER_HALF), PEER_HALF ** -0.5),
        'peer_keys_b': nrm(ks[25], (DEPTH, N_KEYS, PEER_HALF), PEER_HALF ** -0.5),
        'peer_u': nrm(ks[26], (DEPTH, N_EXPERTS, D_MODEL), D_MODEL ** -0.5),
        'peer_v': nrm(ks[27], (DEPTH, N_EXPERTS, D_MODEL), PEER_HEADS ** -0.5),
    }


def reference(x_prompt, x_sample, cache_fox_k, cache_fox_v, cache_fox_logf, cache_diff_k, cache_diff_v,
              page_table, g_attn_norm, w_in, b_forget, g_q_fox, g_k_fox, g_q_diff, g_k_diff,
              lambda_q1, lambda_k1, lambda_q2, lambda_k2, g_diff_sub, w_out, g_ffn_norm,
              w_peer_q, peer_keys_a, peer_keys_b, peer_u, peer_v):
    n_blocks = SEQ // Q_BLOCK
    pos_p = jnp.arange(SEQ, dtype=jnp.int32)
    pos_past = jnp.arange(PAST_LEN, dtype=jnp.int32)
    pos_new = PAST_LEN + jnp.arange(DEC_SEQ, dtype=jnp.int32)

    def to_blocks(a):
        return jnp.moveaxis(a.reshape((a.shape[0], n_blocks, Q_BLOCK) + a.shape[2:]), 1, 0)

    def from_blocks(a):
        return jnp.moveaxis(a, 0, 1).reshape((a.shape[1], n_blocks * Q_BLOCK) + a.shape[3:])

    xp, xs = x_prompt, x_sample
    fk_p, fv_p, fl_p, dk_p, dv_p = [], [], [], [], []
    fk_s, fv_s, fl_s, dk_s, dv_s = [], [], [], [], []
    for l in range(DEPTH):
        lam_init = 0.8 - 0.6 * math.exp(-0.3 * l)
        lam = (jnp.exp(jnp.sum(lambda_q1[l].astype(jnp.float32) * lambda_k1[l].astype(jnp.float32)))
               - jnp.exp(jnp.sum(lambda_q2[l].astype(jnp.float32) * lambda_k2[l].astype(jnp.float32)))
               + lam_init)
        proj = (w_in[l], b_forget[l], g_q_fox[l], g_k_fox[l], g_q_diff[l], g_k_diff[l])
        peer_w = (w_peer_q[l], peer_keys_a[l], peer_keys_b[l], peer_u[l], peer_v[l])

        qf, kf, vf, logf, qd, kd, vd = mixer_inputs(rms_norm(xp, g_attn_norm[l]), *proj, pos_p)
        cf = jnp.cumsum(logf, axis=1)

        def attn_block(args):
            qf_b, cf_b, qd_b, pos_b = args
            of_b = fox_attend(qf_b, cf_b, pos_b, ((kf, vf, cf, pos_p),))
            od_b = diff_attend(qd_b, lam, pos_b, ((kd, vd, pos_p),))
            return of_b, od_b

        of_blk, od_blk = lax.map(attn_block, (to_blocks(qf), to_blocks(cf), to_blocks(qd),
                                              pos_p.reshape(n_blocks, Q_BLOCK)))
        xp = xp + merge_out(from_blocks(of_blk), from_blocks(od_blk), g_diff_sub[l], lam_init, w_out[l])
        xp = xp + peer_ffn(rms_norm(xp, g_ffn_norm[l]), *peer_w)
        fk_p.append(kf); fv_p.append(vf); fl_p.append(logf); dk_p.append(kd); dv_p.append(vd)

        qf_n, kf_n, vf_n, logf_n, qd_n, kd_n, vd_n = mixer_inputs(rms_norm(xs, g_attn_norm[l]), *proj, pos_new)
        pk_f = gather_pages(cache_fox_k, l, page_table)
        pv_f = gather_pages(cache_fox_v, l, page_table)
        pl_f = gather_pages(cache_fox_logf, l, page_table).astype(jnp.float32)
        pk_d = gather_pages(cache_diff_k, l, page_table)
        pv_d = gather_pages(cache_diff_v, l, page_table)
        c_past = jnp.cumsum(pl_f, axis=1)
        c_new = c_past[:, -1:, :] + jnp.cumsum(logf_n, axis=1)
        of_n = fox_attend(qf_n, c_new, pos_new, ((pk_f, pv_f, c_past, pos_past), (kf_n, vf_n, c_new, pos_new)))
        od_n = diff_attend(qd_n, lam, pos_new, ((pk_d, pv_d, pos_past), (kd_n, vd_n, pos_new)))
        xs = xs + merge_out(of_n, od_n, g_diff_sub[l], lam_init, w_out[l])
        xs = xs + peer_ffn(rms_norm(xs, g_ffn_norm[l]), *peer_w)
        fk_s.append(kf_n); fv_s.append(vf_n); fl_s.append(logf_n); dk_s.append(kd_n); dv_s.append(vd_n)

    return (xp, xs,
            jnp.stack(fk_p), jnp.stack(fv_p), jnp.stack(fl_p), jnp.stack(dk_p), jnp.stack(dv_p),
            jnp.stack(fk_s), jnp.stack(fv_s), jnp.stack(fl_s), jnp.stack(dk_s), jnp.stack(dv_s))
```

```python
import functools
import math

import jax
import jax.numpy as jnp
from jax import lax
from jax.experimental import pallas as pl
from jax.experimental.pallas import tpu as pltpu

F32 = jnp.float32
BF16 = jnp.bfloat16

HEAD_DIM = 128
LANES = 128
SUBLANES = 8
ROPE_THETA = 10000.0
ATTN_SCALE = HEAD_DIM ** -0.5
NEG_INF = -1e30
NORM_EPS = 1e-6
SUBLN_EPS = 1e-5
N_KEYS = 128
PEER_HEADS = 8
PEER_TOPK = 16
PEER_PAIRS = PEER_HEADS * PEER_TOPK
LOGF_COPIES = 3
VMEM_LIMIT = 56 * 1024 * 1024

_NT = (((1,), (1,)), ((), ()))


def _params(sem, vmem=VMEM_LIMIT):
    return pltpu.CompilerParams(dimension_semantics=sem, vmem_limit_bytes=vmem)


def _rmsnorm_kernel(x_ref, g_ref, o_ref, *, eps):
    x = x_ref[...]
    ms = jnp.mean(x * x, axis=-1, keepdims=True)
    o_ref[...] = (x * lax.rsqrt(ms + eps) * g_ref[...]).astype(o_ref.dtype)


def rmsnorm(x, g, out_dtype, tm=256):
    n, d = x.shape
    tm = min(tm, n)
    return pl.pallas_call(
        functools.partial(_rmsnorm_kernel, eps=NORM_EPS),
        grid=(n // tm,),
        in_specs=[pl.BlockSpec((tm, d), lambda i: (i, 0)),
                  pl.BlockSpec((1, d), lambda i: (0, 0))],
        out_specs=pl.BlockSpec((tm, d), lambda i: (i, 0)),
        out_shape=jax.ShapeDtypeStruct((n, d), out_dtype),
        compiler_params=_params(("parallel",)),
        name="rmsnorm",
    )(x, g.reshape(1, d))


def _head_rmsnorm(x, g):
    ms = jnp.mean(x * x, axis=-1, keepdims=True)
    return x * lax.rsqrt(ms + NORM_EPS) * g


def _proj_kernel(*refs, mode, n_x):
    x_refs, w_refs = refs[:n_x], refs[n_x:2 * n_x]
    aux, o_ref = refs[2 * n_x:-1], refs[-1]
    acc = jnp.dot(x_refs[0][...], w_refs[0][...], preferred_element_type=F32)
    for x_ref, w_ref in zip(x_refs[1:], w_refs[1:]):
        acc = acc + jnp.dot(x_ref[...], w_ref[...], preferred_element_type=F32)
    tn = acc.shape[1]
    if mode == "plain":
        o_ref[...] = acc
    elif mode == "residual":
        o_ref[...] = aux[0][...] + acc
    elif mode == "logsig":
        z = acc + aux[0][...]
        o_ref[...] = jnp.minimum(z, 0.0) - jnp.log1p(jnp.exp(-jnp.abs(z)))
    else:
        g = aux[0][...]
        for h in range(tn // HEAD_DIM):
            sl = slice(h * HEAD_DIM, (h + 1) * HEAD_DIM)
            y = _head_rmsnorm(acc[:, sl], g)
            if mode == "headnorm_rope":
                y = y * aux[1][...] + pltpu.roll(y, HEAD_DIM // 2, 1) * aux[2][...]
            o_ref[:, sl] = y


def proj(xs, ws, mode, aux=(), aux_kinds=(), tm=512, tn=512):
    n = xs[0].shape[0]
    ncols = ws[0].shape[1]
    tm, tn = min(tm, n), min(tn, ncols)
    in_specs = [pl.BlockSpec((tm, x.shape[1]), lambda j, i: (i, 0)) for x in xs]
    in_specs += [pl.BlockSpec((w.shape[0], tn), lambda j, i: (0, j)) for w in ws]
    for kind in aux_kinds:
        if kind == "col":
            in_specs.append(pl.BlockSpec((1, HEAD_DIM), lambda j, i: (0, 0)))
        elif kind == "row":
            in_specs.append(pl.BlockSpec((tm, HEAD_DIM), lambda j, i: (i, 0)))
        elif kind == "bias":
            in_specs.append(pl.BlockSpec((1, tn), lambda j, i: (0, j)))
        else:
            in_specs.append(pl.BlockSpec((tm, tn), lambda j, i: (i, j)))
    return pl.pallas_call(
        functools.partial(_proj_kernel, mode=mode, n_x=len(xs)),
        grid=(ncols // tn, n // tm),
        in_specs=in_specs,
        out_specs=pl.BlockSpec((tm, tn), lambda j, i: (i, j)),
        out_shape=jax.ShapeDtypeStruct((n, ncols), F32),
        compiler_params=_params(("parallel", "parallel")),
        name="proj_" + mode,
    )(*xs, *ws, *aux)


def _cumsum_kernel(lf_ref, c_ref, ct_ref, *, blk):
    s = lf_ref.shape[0]
    r = lax.broadcasted_iota(jnp.int32, (blk, blk), 0)
    c = lax.broadcasted_iota(jnp.int32, (blk, blk), 1)
    tri = jnp.where(r >= c, 1.0, 0.0).astype(F32)
    carry = jnp.zeros((1, LANES), F32)
    for b in range(s // blk):
        x = lf_ref[b * blk:(b + 1) * blk, :]
        cs = jnp.dot(tri, x, preferred_element_type=F32, precision=lax.Precision.HIGHEST) + carry
        c_ref[b * blk:(b + 1) * blk, :] = cs
        ct_ref[:, b * blk:(b + 1) * blk] = cs.T
        carry = cs[blk - 1:blk, :]


def cumsum_logf(lf, batch, seq):
    return pl.pallas_call(
        functools.partial(_cumsum_kernel, blk=LANES),
        grid=(batch,),
        in_specs=[pl.BlockSpec((seq, LANES), lambda b: (b, 0))],
        out_specs=[pl.BlockSpec((seq, LANES), lambda b: (b, 0)),
                   pl.BlockSpec((LANES, seq), lambda b: (b, 0))],
        out_shape=[jax.ShapeDtypeStruct((batch * seq, LANES), F32),
                   jax.ShapeDtypeStruct((batch * LANES, seq), F32)],
        compiler_params=_params(("parallel",)),
        name="cumsum_logf",
    )(lf)


def _flash_rows(q_bf, k_scr, v_scr, n_kv, q_pos0, tk, bias):
    tq = q_bf.shape[0]
    dv = v_scr.shape[1]
    row = q_pos0 + lax.broadcasted_iota(jnp.int32, (tq, tk), 0)
    col0 = lax.broadcasted_iota(jnp.int32, (tq, tk), 1)

    def body(kj, carry):
        m, l, acc = carry
        start = pl.multiple_of(kj * tk, tk)
        s = lax.dot_general(q_bf, k_scr[pl.ds(start, tk), :], _NT, preferred_element_type=F32) * ATTN_SCALE
        if bias is not None:
            s = s + bias[0] - bias[1][kj][0:1, :]
        s = jnp.where(col0 + kj * tk <= row, s, NEG_INF)
        m_new = jnp.maximum(m, jnp.max(s, axis=-1, keepdims=True))
        alpha = jnp.exp(m - m_new)
        p = jnp.exp(s - m_new)
        l = alpha * l + jnp.sum(p, axis=-1, keepdims=True)
        acc = alpha * acc + jnp.dot(p.astype(BF16), v_scr[pl.ds(start, tk), :], preferred_element_type=F32)
        return m_new, l, acc

    init = (jnp.full((tq, 1), NEG_INF, F32), jnp.zeros((tq, 1), F32), jnp.zeros((tq, dv), F32))
    _, l, acc = lax.fori_loop(0, n_kv, body, init)
    return acc / l


def _fox_prompt_kernel(q_ref, k_ref, v_ref, c_ref, ct_ref, o_ref, k_scr, v_scr, ck_scr, *, tq, tk):
    h = pl.program_id(1)
    s = q_ref.shape[0]
    k_scr[...] = k_ref[...].astype(BF16)
    v_scr[...] = v_ref[...].astype(BF16)
    lane = lax.broadcasted_iota(jnp.int32, (s, LANES), 1)
    cq = jnp.sum(jnp.where(lane == h, c_ref[...], 0.0), axis=-1, keepdims=True)
    ck_row = ct_ref[pl.ds(h, 1), :]
    for j in range(s // tk):
        ck_scr[j] = jnp.broadcast_to(ck_row[:, j * tk:(j + 1) * tk], (SUBLANES, tk))
    for qi in range(s // tq):
        rows = slice(qi * tq, (qi + 1) * tq)
        n_kv = (qi * tq + tq + tk - 1) // tk
        o = _flash_rows(q_ref[rows, :].astype(BF16), k_scr, v_scr, n_kv, qi * tq, tk, (cq[rows], ck_scr))
        o_ref[rows, :] = o.astype(o_ref.dtype)


def fox_prompt_attention(qf, kf, vf, c, ct, batch, seq, heads, tq=256, tk=256):
    blk = pl.BlockSpec((seq, HEAD_DIM), lambda b, h: (b, h))
    return pl.pallas_call(
        functools.partial(_fox_prompt_kernel, tq=tq, tk=tk),
        grid=(batch, heads),
        in_specs=[blk, blk, blk,
                  pl.BlockSpec((seq, LANES), lambda b, h: (b, 0)),
                  pl.BlockSpec((LANES, seq), lambda b, h: (b, 0))],
        out_specs=blk,
        out_shape=jax.ShapeDtypeStruct((batch * seq, heads * HEAD_DIM), BF16),
        scratch_shapes=[pltpu.VMEM((seq, HEAD_DIM), BF16), pltpu.VMEM((seq, HEAD_DIM), BF16),
                        pltpu.VMEM((seq // tk, SUBLANES, tk), F32)],
        compiler_params=_params(("parallel", "parallel")),
        name="fox_prompt_attention",
    )(qf, kf, vf, c, ct)


def _lambda_value(lq1, lk1, lq2, lk2, lam_init):
    a = jnp.sum(lq1[...] * lk1[...], axis=-1, keepdims=True)
    b = jnp.sum(lq2[...] * lk2[...], axis=-1, keepdims=True)
    return jnp.exp(a) - jnp.exp(b) + lam_init


def _sub_layer_norm(od, g, lam_init):
    ms = jnp.mean(od * od, axis=-1, keepdims=True)
    return (od * lax.rsqrt(ms + SUBLN_EPS) * g) * (1.0 - lam_init)


def _diff_prompt_kernel(q_ref, k_ref, v_ref, lq1, lk1, lq2, lk2, g_ref, o_ref, k_scr, v_scr, *, tq, tk, lam_init):
    s = q_ref.shape[0]
    lam = _lambda_value(lq1, lk1, lq2, lk2, lam_init)
    k_scr[0] = k_ref[:, :HEAD_DIM].astype(BF16)
    k_scr[1] = k_ref[:, HEAD_DIM:].astype(BF16)
    v_scr[...] = v_ref[...].astype(BF16)
    for qi in range(s // tq):
        rows = slice(qi * tq, (qi + 1) * tq)
        n_kv = (qi * tq + tq + tk - 1) // tk
        o1 = _flash_rows(q_ref[rows, :HEAD_DIM].astype(BF16), k_scr.at[0], v_scr, n_kv, qi * tq, tk, None)
        o2 = _flash_rows(q_ref[rows, HEAD_DIM:].astype(BF16), k_scr.at[1], v_scr, n_kv, qi * tq, tk, None)
        o_ref[rows, :] = _sub_layer_norm(o1 - lam * o2, g_ref[...], lam_init).astype(o_ref.dtype)


def diff_prompt_attention(qd, kd, vd, lams, g_sub, lam_init, batch, seq, heads, tq=256, tk=256):
    blk = pl.BlockSpec((seq, 2 * HEAD_DIM), lambda b, h: (b, h))
    vec = pl.BlockSpec((1, HEAD_DIM), lambda b, h: (0, 0))
    return pl.pallas_call(
        functools.partial(_diff_prompt_kernel, tq=tq, tk=tk, lam_init=lam_init),
        grid=(batch, heads),
        in_specs=[blk, blk, blk, vec, vec, vec, vec,
                  pl.BlockSpec((1, 2 * HEAD_DIM), lambda b, h: (0, 0))],
        out_specs=blk,
        out_shape=jax.ShapeDtypeStruct((batch * seq, heads * 2 * HEAD_DIM), BF16),
        scratch_shapes=[pltpu.VMEM((2, seq, HEAD_DIM), BF16), pltpu.VMEM((seq, 2 * HEAD_DIM), BF16)],
        compiler_params=_params(("parallel", "parallel")),
        name="diff_prompt_attention",
    )(qd, kd, vd, *lams, g_sub)


def _online_update(s, mask, v_bf, m_scr, l_scr, acc_scr):
    s = jnp.where(mask, s, NEG_INF)
    m_prev = m_scr[...]
    m_new = jnp.maximum(m_prev, jnp.max(s, axis=-1, keepdims=True))
    alpha = jnp.exp(m_prev - m_new)
    p = jnp.exp(s - m_new)
    l_scr[...] = alpha * l_scr[...] + jnp.sum(p, axis=-1, keepdims=True)
    acc_scr[...] = alpha * acc_scr[...] + jnp.dot(p.astype(BF16), v_bf, preferred_element_type=F32)
    m_scr[...] = m_new


def _split3(x, lane):
    hi = x.astype(BF16).astype(F32)
    r1 = x - hi
    mid = r1.astype(BF16).astype(F32)
    lo = r1 - mid
    n = LANES // SUBLANES
    return jnp.where(lane < n, hi, jnp.where(lane < 2 * n, mid, jnp.where(lane < 3 * n, lo, 0.0)))


def _rows_by_piece(x, width, order):
    return jnp.concatenate([x[:, j * width:(j + 1) * width] for j in order], axis=0)


def _fox_decode_kernel(pt_ref, q_ref, kn_ref, vn_ref, lfn_ref, kp_ref, vp_ref, lfp_ref, o_ref,
                       q_scr, m_scr, l_scr, acc_scr, later_scr, *, heads):
    del pt_ref
    j = pl.program_id(1)
    tnew = q_ref.shape[0]
    nrow = heads * tnew
    row = lax.broadcasted_iota(jnp.int32, (nrow, LANES), 0)
    lane = lax.broadcasted_iota(jnp.int32, (nrow, LANES), 1)
    order = range(heads)

    @pl.when(j == 0)
    def _():
        qm = _rows_by_piece(q_ref[...], HEAD_DIM, order) * ATTN_SCALE
        onehot = jnp.where((lane < LOGF_COPIES * heads) & ((lane & (heads - 1)) == (row >> 3)), 1.0, 0.0)
        q_scr[...] = jnp.concatenate([qm, onehot], axis=1).astype(BF16)
        m_scr[...] = jnp.full(m_scr.shape, NEG_INF, F32)
        l_scr[...] = jnp.zeros(l_scr.shape, F32)
        acc_scr[...] = jnp.zeros(acc_scr.shape, F32)
        later_scr[...] = jnp.zeros(later_scr.shape, F32)
        lfn = jnp.concatenate([lfn_ref[...], jnp.zeros((LANES - tnew, LANES), F32)], axis=0)
        tri = jnp.where(row >= lane, 1.0, 0.0).astype(F32)
        cn = jnp.dot(tri, lfn, preferred_element_type=F32, precision=lax.Precision.HIGHEST)[0:tnew]
        bn = _split3(-cn, lane[0:tnew])
        kn = _rows_by_piece(kn_ref[...], HEAD_DIM, order)
        kaug = jnp.concatenate([kn, jnp.concatenate([bn] * heads, axis=0)], axis=1).astype(BF16)
        s = lax.dot_general(q_scr[...], kaug, _NT, preferred_element_type=F32)
        mask = ((lane >> 3) == (row >> 3)) & ((lane & 7) <= (row & 7))
        _online_update(s, mask, _rows_by_piece(vn_ref[...], HEAD_DIM, order).astype(BF16), m_scr, l_scr, acc_scr)

    lf = lfp_ref[0]
    triu = jnp.where(row < lane, 1.0, 0.0).astype(F32)
    suf = jnp.dot(triu, lf, preferred_element_type=F32, precision=lax.Precision.HIGHEST) + later_scr[...]
    later_scr[...] = later_scr[...] + jnp.sum(lf, axis=0, keepdims=True)
    bk = _split3(suf, lane)
    page = bk.shape[0]
    bk_rows = jnp.broadcast_to(bk[:, None, :], (page, heads, LANES)).reshape(page * heads, LANES)
    kaug = jnp.concatenate([kp_ref[0].astype(BF16), bk_rows.astype(BF16)], axis=1)
    s = lax.dot_general(q_scr[...], kaug, _NT, preferred_element_type=F32)
    col = lax.broadcasted_iota(jnp.int32, s.shape, 1)
    rw = lax.broadcasted_iota(jnp.int32, s.shape, 0)
    _online_update(s, (col & (heads - 1)) == (rw >> 3), vp_ref[0].astype(BF16), m_scr, l_scr, acc_scr)

    @pl.when(j == pl.num_programs(1) - 1)
    def _():
        o = acc_scr[...] / l_scr[...]
        o_ref[...] = jnp.concatenate([o[h * tnew:(h + 1) * tnew] for h in range(heads)], axis=1)


def fox_decode_attention(page_table, qf, kf, vf, lfn, cache_k, cache_v, cache_lf3, dec_batch, tnew, heads):
    n_pages = page_table.shape[1]
    width = heads * HEAD_DIM
    rows_kv = cache_k.shape[1]
    page = cache_lf3.shape[1]
    tok = pl.BlockSpec((tnew, width), lambda b, j, pt: (b, 0))

    def pg(b, j, pt):
        return (pt[b * n_pages + n_pages - 1 - j], 0, 0)

    grid_spec = pltpu.PrefetchScalarGridSpec(
        num_scalar_prefetch=1,
        grid=(dec_batch, n_pages),
        in_specs=[tok, tok, tok,
                  pl.BlockSpec((tnew, LANES), lambda b, j, pt: (b, 0)),
                  pl.BlockSpec((1, rows_kv, HEAD_DIM), pg),
                  pl.BlockSpec((1, rows_kv, HEAD_DIM), pg),
                  pl.BlockSpec((1, page, LANES), pg)],
        out_specs=tok,
        scratch_shapes=[pltpu.VMEM((heads * tnew, 2 * HEAD_DIM), BF16),
                        pltpu.VMEM((heads * tnew, 1), F32), pltpu.VMEM((heads * tnew, 1), F32),
                        pltpu.VMEM((heads * tnew, HEAD_DIM), F32), pltpu.VMEM((1, LANES), F32)])
    return pl.pallas_call(
        functools.partial(_fox_decode_kernel, heads=heads),
        grid_spec=grid_spec,
        out_shape=jax.ShapeDtypeStruct((dec_batch * tnew, width), F32),
        compiler_params=_params(("parallel", "arbitrary")),
        name="fox_decode_attention",
    )(page_table.reshape(-1), qf, kf, vf, lfn, cache_k, cache_v, cache_lf3)


def _diff_decode_kernel(pt_ref, q_ref, kn_ref, vn_ref, lq1, lk1, lq2, lk2, g_ref, kp_ref, vp_ref, o_ref,
                        q_scr, m_scr, l_scr, acc_scr, *, heads, lam_init):
    del pt_ref
    j = pl.program_id(1)
    tnew = q_ref.shape[0]
    half = heads * tnew
    nrow = 2 * half
    order = [2 * h for h in range(heads)] + [2 * h + 1 for h in range(heads)]

    @pl.when(j == 0)
    def _():
        row = lax.broadcasted_iota(jnp.int32, (nrow, nrow), 0)
        lane = lax.broadcasted_iota(jnp.int32, (nrow, nrow), 1)
        q_scr[...] = (_rows_by_piece(q_ref[...], HEAD_DIM, order) * ATTN_SCALE).astype(BF16)
        m_scr[...] = jnp.full(m_scr.shape, NEG_INF, F32)
        l_scr[...] = jnp.zeros(l_scr.shape, F32)
        acc_scr[...] = jnp.zeros(acc_scr.shape, F32)
        kn = _rows_by_piece(kn_ref[...], HEAD_DIM, order).astype(BF16)
        s = lax.dot_general(q_scr[...], kn, _NT, preferred_element_type=F32)
        mask = ((lane >> 3) == (row >> 3)) & ((lane & 7) <= (row & 7))
        vn = _rows_by_piece(vn_ref[...], 2 * HEAD_DIM, [jj // 2 for jj in order]).astype(BF16)
        _online_update(s, mask, vn, m_scr, l_scr, acc_scr)

    n_keys = vp_ref.shape[1]
    s = jnp.concatenate(
        [lax.dot_general(q_scr[c * half:(c + 1) * half, :],
                         kp_ref[0, pl.ds(c, n_keys, stride=2), :].astype(BF16), _NT, preferred_element_type=F32)
         for c in range(2)], axis=0)
    col = lax.broadcasted_iota(jnp.int32, s.shape, 1)
    rw = lax.broadcasted_iota(jnp.int32, s.shape, 0)
    _online_update(s, (col & (heads - 1)) == ((rw & (half - 1)) >> 3), vp_ref[0].astype(BF16), m_scr, l_scr, acc_scr)

    @pl.when(j == pl.num_programs(1) - 1)
    def _():
        lam = _lambda_value(lq1, lk1, lq2, lk2, lam_init)
        o = acc_scr[...] / l_scr[...]
        y = _sub_layer_norm(o[:half] - lam * o[half:], g_ref[...], lam_init)
        o_ref[...] = jnp.concatenate([y[h * tnew:(h + 1) * tnew] for h in range(heads)], axis=1)


def diff_decode_attention(page_table, qd, kd, vd, lams, g_sub, lam_init, cache_k, cache_v, dec_batch, tnew, heads):
    n_pages = page_table.shape[1]
    width = heads * 2 * HEAD_DIM
    tok = pl.BlockSpec((tnew, width), lambda b, j, pt: (b, 0))
    vec = pl.BlockSpec((1, HEAD_DIM), lambda b, j, pt: (0, 0))

    def pg(b, j, pt):
        return (pt[b * n_pages + j], 0, 0)

    nrow = 2 * heads * tnew
    grid_spec = pltpu.PrefetchScalarGridSpec(
        num_scalar_prefetch=1,
        grid=(dec_batch, n_pages),
        in_specs=[tok, tok, tok, vec, vec, vec, vec,
                  pl.BlockSpec((1, 2 * HEAD_DIM), lambda b, j, pt: (0, 0)),
                  pl.BlockSpec((1,) + cache_k.shape[1:], pg),
                  pl.BlockSpec((1,) + cache_v.shape[1:], pg)],
        out_specs=tok,
        scratch_shapes=[pltpu.VMEM((nrow, HEAD_DIM), BF16),
                        pltpu.VMEM((nrow, 1), F32), pltpu.VMEM((nrow, 1), F32),
                        pltpu.VMEM((nrow, 2 * HEAD_DIM), F32)])
    return pl.pallas_call(
        functools.partial(_diff_decode_kernel, heads=heads, lam_init=lam_init),
        grid_spec=grid_spec,
        out_shape=jax.ShapeDtypeStruct((dec_batch * tnew, width), F32),
        compiler_params=_params(("parallel", "arbitrary")),
        name="diff_decode_attention",
    )(page_table.reshape(-1), qd, kd, vd, *lams, g_sub, cache_k, cache_v)


def _topk_rows(s, payload=None):
    n, t = s.shape
    rid = lax.broadcasted_iota(jnp.int32, (n, t), 0).astype(F32)
    slot = lax.broadcasted_iota(jnp.int32, (PEER_TOPK, t), 0)
    vals = jnp.zeros((PEER_TOPK, t), F32)
    idxs = jnp.zeros((PEER_TOPK, t), F32)
    pays = jnp.zeros((PEER_TOPK, t), F32)
    for r in range(PEER_TOPK):
        m = jnp.max(s, axis=0, keepdims=True)
        idx = jnp.min(jnp.where(s == m, rid, float(n)), axis=0, keepdims=True)
        hit = rid == idx
        vals = jnp.where(slot == r, m, vals)
        idxs = jnp.where(slot == r, idx, idxs)
        if payload is not None:
            pays = jnp.where(slot == r, jnp.sum(jnp.where(hit, payload, 0.0), axis=0, keepdims=True), pays)
        s = jnp.where(hit, -jnp.inf, s)
    return vals, idxs, pays


def _router_kernel(h_ref, wq_ref, ka_ref, kb_ref, g_ref, e_ref):
    t = h_ref.shape[0]
    q = jnp.dot(h_ref[...].astype(BF16), wq_ref[...], preferred_element_type=F32)
    ka = ka_ref[...].astype(BF16)
    kb = kb_ref[...].astype(BF16)
    half = ka.shape[1]
    for h in range(PEER_HEADS):
        qa = q[:, (2 * h) * half:(2 * h + 1) * half].astype(BF16)
        qb = q[:, (2 * h + 1) * half:(2 * h + 2) * half].astype(BF16)
        sa = lax.dot_general(ka, qa, _NT, preferred_element_type=F32)
        sb = lax.dot_general(kb, qb, _NT, preferred_element_type=F32)
        va, ia, _ = _topk_rows(sa)
        vb, ib, _ = _topk_rows(sb)
        rep = lambda x: jnp.concatenate(
            [jnp.broadcast_to(x[i:i + 1, :], (PEER_TOPK, t)) for i in range(PEER_TOPK)], axis=0)
        tile = lambda x: jnp.concatenate([x] * PEER_TOPK, axis=0)
        cand = rep(va) + tile(vb)
        cidx = rep(ia) * float(N_KEYS) + tile(ib)
        sv, _, e = _topk_rows(cand, cidx)
        ex = jnp.exp(sv - sv[0:1, :])
        g_ref[h * PEER_TOPK:(h + 1) * PEER_TOPK, :] = ex / jnp.sum(ex, axis=0, keepdims=True)
        e_ref[h * PEER_TOPK:(h + 1) * PEER_TOPK, :] = e.astype(jnp.int32)


def peer_router(h, wq, keys_a, keys_b, tm=256):
    n, d = h.shape
    tm = min(tm, n)
    return pl.pallas_call(
        _router_kernel,
        grid=(n // tm,),
        in_specs=[pl.BlockSpec((tm, d), lambda i: (i, 0)),
                  pl.BlockSpec(wq.shape, lambda i: (0, 0)),
                  pl.BlockSpec(keys_a.shape, lambda i: (0, 0)),
                  pl.BlockSpec(keys_b.shape, lambda i: (0, 0))],
        out_specs=[pl.BlockSpec((PEER_PAIRS, tm), lambda i: (0, i)),
                   pl.BlockSpec((PEER_PAIRS, tm), lambda i: (0, i))],
        out_shape=[jax.ShapeDtypeStruct((PEER_PAIRS, n), F32),
                   jax.ShapeDtypeStruct((PEER_PAIRS, n), jnp.int32)],
        compiler_params=_params(("parallel",)),
        name="peer_router",
    )(h, wq, keys_a, keys_b)


def _peer_expert_kernel(idx_ref, h_ref, x_ref, g_ref, u_hbm, v_hbm, o_ref, ubuf, vbuf, sem, xb_scr, part_scr):
    tb, d = h_ref.shape
    n_chunks = d // LANES
    groups = PEER_PAIRS // SUBLANES

    def row_copy(table, buf, which, slot, e, k):
        return pltpu.make_async_copy(table.at[pl.ds(e, 1), :], buf.at[slot, pl.ds(k, 1), :], sem.at[which, slot])

    def start(t, slot):
        for k in range(PEER_PAIRS):
            e = idx_ref[t, k]
            row_copy(u_hbm, ubuf, 0, slot, e, k).start()
            row_copy(v_hbm, vbuf, 1, slot, e, k).start()

    def wait(slot):
        pltpu.make_async_copy(u_hbm.at[pl.ds(0, PEER_PAIRS), :], ubuf.at[slot], sem.at[0, slot]).wait()
        pltpu.make_async_copy(v_hbm.at[pl.ds(0, PEER_PAIRS), :], vbuf.at[slot], sem.at[1, slot]).wait()

    lane = lax.broadcasted_iota(jnp.int32, (PEER_PAIRS, tb), 1)
    start(0, 0)

    def body(t, _):
        slot = t & 1

        @pl.when(t + 1 < tb)
        def _():
            start(t + 1, 1 - slot)

        wait(slot)
        xb_scr[...] = jnp.broadcast_to(h_ref[pl.ds(t, 1), :], (SUBLANES, d))
        acc = jnp.zeros((groups, SUBLANES, LANES), F32)
        for c in range(n_chunks):
            sl = slice(c * LANES, (c + 1) * LANES)
            acc = acc + ubuf[slot, :, sl].reshape(groups, SUBLANES, LANES) * xb_scr[:, sl][None]
        act = jnp.sum(acc.reshape(PEER_PAIRS, LANES), axis=-1, keepdims=True)
        gate = jnp.sum(jnp.where(lane == t, g_ref[...], 0.0), axis=-1, keepdims=True)
        w = gate * (0.5 * act * (1.0 + lax.erf(act * math.sqrt(0.5))))
        wb = jnp.broadcast_to(w, (PEER_PAIRS, LANES)).reshape(groups, SUBLANES, LANES)
        for c in range(n_chunks):
            sl = slice(c * LANES, (c + 1) * LANES)
            part_scr[:, sl] = jnp.sum(vbuf[slot, :, sl].reshape(groups, SUBLANES, LANES) * wb, axis=0)
        o_ref[pl.ds(t, 1), :] = x_ref[pl.ds(t, 1), :] + jnp.sum(part_scr[...], axis=0, keepdims=True)
        return 0

    lax.fori_loop(0, tb, body, 0)


def peer_experts(eidx, h, x, gates, u, v, tb=128):
    n, d = h.shape
    tb = min(tb, n)
    tok = pl.BlockSpec((tb, d), lambda i: (i, 0))
    return pl.pallas_call(
        _peer_expert_kernel,
        grid=(n // tb,),
        in_specs=[pl.BlockSpec((tb, PEER_PAIRS), lambda i: (i, 0), memory_space=pltpu.SMEM),
                  tok, tok,
                  pl.BlockSpec((PEER_PAIRS, tb), lambda i: (0, i)),
                  pl.BlockSpec(memory_space=pl.ANY),
                  pl.BlockSpec(memory_space=pl.ANY)],
        out_specs=tok,
        out_shape=jax.ShapeDtypeStruct((n, d), F32),
        scratch_shapes=[pltpu.VMEM((2, PEER_PAIRS, d), F32), pltpu.VMEM((2, PEER_PAIRS, d), F32),
                        pltpu.SemaphoreType.DMA((2, 2)),
                        pltpu.VMEM((SUBLANES, d), F32), pltpu.VMEM((SUBLANES, d), F32)],
        compiler_params=_params(("arbitrary",)),
        name="peer_experts",
    )(eidx, h, x, gates, u, v)


def _rope_tables(pos):
    inv = ROPE_THETA ** (-jnp.arange(0, HEAD_DIM, 2, dtype=F32) / HEAD_DIM)
    ang = pos.astype(F32)[:, None] * inv[None, :]
    cos, sin = jnp.cos(ang), jnp.sin(ang)
    return jnp.concatenate([cos, cos], axis=-1), jnp.concatenate([-sin, sin], axis=-1)


def _mixer_inputs(x, pos, w, tm):
    h = rmsnorm(x, w["g_attn"], BF16, tm=min(tm, 256))
    cos, sin = _rope_tables(pos)
    hn = functools.partial(proj, [h], mode="headnorm", aux_kinds=("col",), tm=tm)
    hr = functools.partial(proj, [h], mode="headnorm_rope", aux_kinds=("col", "row", "row"), tm=tm)
    qf = hn([w["w_fq"]], aux=(w["g_qf"],))
    kf = hn([w["w_fk"]], aux=(w["g_kf"],))
    vf = proj([h], [w["w_fv"]], "plain", tm=tm)
    lf3 = proj([h], [w["w_ff3"]], "logsig", aux=(w["b_ff3"],), aux_kinds=("bias",), tm=tm)
    qd = hr([w["w_dq"]], aux=(w["g_qd"], cos, sin))
    kd = hr([w["w_dk"]], aux=(w["g_kd"], cos, sin))
    vd = proj([h], [w["w_dv"]], "plain", tm=tm)
    return qf, kf, vf, lf3, qd, kd, vd


def _merge_and_ffn(x, of, od, w, tm):
    x1 = proj([of, od], [w["w_out_f"], w["w_out_d"]], "residual", aux=(x,), aux_kinds=("full",), tm=tm)
    h2 = rmsnorm(x1, w["g_ffn"], F32, tm=min(tm, 256))
    gates, eidx_t = peer_router(h2, w["w_peer_q"], w["keys_a"], w["keys_b"])
    return peer_experts(eidx_t.T, h2, x1, gates, w["peer_u"], w["peer_v"])


def kernel(x_prompt, x_sample, cache_fox_k, cache_fox_v, cache_fox_logf, cache_diff_k, cache_diff_v, page_table, g_attn_norm, w_in, b_forget, g_q_fox, g_k_fox, g_q_diff, g_k_diff, lambda_q1, lambda_k1, lambda_q2, lambda_k2, g_diff_sub, w_out, g_ffn_norm, w_peer_q, peer_keys_a, peer_keys_b, peer_u, peer_v):
    batch, seq, d_model = x_prompt.shape
    dec_batch, dec_seq, _ = x_sample.shape
    depth, n_pool, page_size, fox_heads, _ = cache_fox_k.shape
    diff_heads = cache_diff_k.shape[3]
    fox_w = fox_heads * HEAD_DIM
    diff_w = diff_heads * 2 * HEAD_DIM
    past_len = page_table.shape[1] * page_size
    assert fox_heads * LOGF_COPIES <= LANES and dec_seq == SUBLANES

    xp = x_prompt.reshape(batch * seq, d_model)
    xs = x_sample.reshape(dec_batch * dec_seq, d_model)
    pos_p = jnp.tile(jnp.arange(seq, dtype=jnp.int32), batch)
    pos_s = jnp.tile(past_len + jnp.arange(dec_seq, dtype=jnp.int32), dec_batch)
    outs_p, outs_s = [], []
    for l in range(depth):
        lam_init = 0.8 - 0.6 * math.exp(-0.3 * l)
        wi = w_in[l]
        o = 0
        sec = {}
        for name, width in (("w_fq", fox_w), ("w_fk", fox_w), ("w_fv", fox_w), ("w_ff", fox_heads),
                            ("w_dq", diff_w), ("w_dk", diff_w), ("w_dv", diff_w)):
            sec[name] = wi[:, o:o + width]
            o += width
        w_ff = sec.pop("w_ff")
        pad = LANES - LOGF_COPIES * fox_heads
        w = {k: v.astype(BF16) for k, v in sec.items()}
        w["w_ff3"] = jnp.pad(jnp.tile(w_ff, (1, LOGF_COPIES)), ((0, 0), (0, pad))).astype(BF16)
        w["b_ff3"] = jnp.pad(jnp.tile(b_forget[l], LOGF_COPIES), (0, pad)).reshape(1, LANES)
        w["g_attn"], w["g_ffn"] = g_attn_norm[l], g_ffn_norm[l]
        for name, g in (("g_qf", g_q_fox), ("g_kf", g_k_fox), ("g_qd", g_q_diff), ("g_kd", g_k_diff)):
            w[name] = g[l].reshape(1, HEAD_DIM)
        w["w_out_f"] = w_out[l, :fox_w].astype(BF16)
        w["w_out_d"] = w_out[l, fox_w:].astype(BF16)
        w["w_peer_q"] = w_peer_q[l].astype(BF16)
        w["keys_a"], w["keys_b"] = peer_keys_a[l], peer_keys_b[l]
        w["peer_u"], w["peer_v"] = peer_u[l], peer_v[l]
        lams = tuple(a[l].reshape(1, HEAD_DIM) for a in (lambda_q1, lambda_k1, lambda_q2, lambda_k2))
        g_sub = g_diff_sub[l].reshape(1, 2 * HEAD_DIM)

        qf, kf, vf, lf3, qd, kd, vd = _mixer_inputs(xp, pos_p, w, tm=512)
        c, ct = cumsum_logf(lf3, batch, seq)
        of = fox_prompt_attention(qf, kf, vf, c, ct, batch, seq, fox_heads)
        od = diff_prompt_attention(qd, kd, vd, lams, g_sub, lam_init, batch, seq, diff_heads)
        xp = _merge_and_ffn(xp, of, od, w, tm=512)
        outs_p.append((kf, vf, lf3[:, :fox_heads], kd, vd))

        qf, kf, vf, lf3, qd, kd, vd = _mixer_inputs(xs, pos_s, w, tm=256)
        lf_cache = cache_fox_logf[l].astype(F32)
        lf_cache3 = jnp.pad(jnp.tile(lf_cache, (1, 1, LOGF_COPIES)), ((0, 0), (0, 0), (0, pad)))
        of = fox_decode_attention(page_table, qf, kf, vf, lf3,
                                  cache_fox_k[l].reshape(n_pool, page_size * fox_heads, HEAD_DIM),
                                  cache_fox_v[l].reshape(n_pool, page_size * fox_heads, HEAD_DIM),
                                  lf_cache3, dec_batch, dec_seq, fox_heads)
        od = diff_decode_attention(page_table, qd, kd, vd, lams, g_sub, lam_init,
                                   cache_diff_k[l].reshape(n_pool, page_size * diff_heads * 2, HEAD_DIM),
                                   cache_diff_v[l].reshape(n_pool, page_size * diff_heads, 2 * HEAD_DIM),
                                   dec_batch, dec_seq, diff_heads)
        xs = _merge_and_ffn(xs, of.astype(BF16), od.astype(BF16), w, tm=256)
        outs_s.append((kf, vf, lf3[:, :fox_heads], kd, vd))

    def stack(outs, i, shape):
        return jnp.stack([o[i].reshape(shape) for o in outs])

    res = [xp.reshape(batch, seq, d_model), xs.reshape(dec_batch, dec_seq, d_model)]
    for outs, b, t in ((outs_p, batch, seq), (outs_s, dec_batch, dec_seq)):
        res += [stack(outs, 0, (b, t, fox_heads, HEAD_DIM)), stack(outs, 1, (b, t, fox_heads, HEAD_DIM)),
                stack(outs, 2, (b, t, fox_heads)), stack(outs, 3, (b, t, diff_heads, 2, HEAD_DIM)),
                stack(outs, 4, (b, t, diff_heads, 2 * HEAD_DIM))]
    return tuple(res)
```

```python
import functools
import math

import jax
import jax.numpy as jnp
from jax import lax
from jax.experimental import pallas as pl
from jax.experimental.pallas import tpu as pltpu

F32 = jnp.float32
BF16 = jnp.bfloat16

HEAD_DIM = 128
LANES = 128
SUBLANES = 8
ROPE_THETA = 10000.0
ATTN_SCALE = HEAD_DIM ** -0.5
NEG_INF = -1e30
NORM_EPS = 1e-6
SUBLN_EPS = 1e-5
N_KEYS = 128
PEER_HEADS = 8
PEER_TOPK = 16
PEER_PAIRS = PEER_HEADS * PEER_TOPK
LOGF_COPIES = 3
DECODE_GROUP = 2
VMEM_LIMIT = 56 * 1024 * 1024

_NT = (((1,), (1,)), ((), ()))


def _params(sem, vmem=VMEM_LIMIT):
    return pltpu.CompilerParams(dimension_semantics=sem, vmem_limit_bytes=vmem)


def _rmsnorm_kernel(x_ref, g_ref, o_ref, *, eps):
    x = x_ref[...]
    ms = jnp.mean(x * x, axis=-1, keepdims=True)
    o_ref[...] = (x * lax.rsqrt(ms + eps) * g_ref[...]).astype(o_ref.dtype)


def rmsnorm(x, g, out_dtype, tm=256):
    n, d = x.shape
    tm = min(tm, n)
    return pl.pallas_call(
        functools.partial(_rmsnorm_kernel, eps=NORM_EPS),
        grid=(n // tm,),
        in_specs=[pl.BlockSpec((tm, d), lambda i: (i, 0)),
                  pl.BlockSpec((1, d), lambda i: (0, 0))],
        out_specs=pl.BlockSpec((tm, d), lambda i: (i, 0)),
        out_shape=jax.ShapeDtypeStruct((n, d), out_dtype),
        compiler_params=_params(("parallel",)),
        name="rmsnorm",
    )(x, g.reshape(1, d))


def _head_rmsnorm(x, g):
    ms = jnp.mean(x * x, axis=-1, keepdims=True)
    return x * lax.rsqrt(ms + NORM_EPS) * g


def _proj_kernel(*refs, mode, n_x):
    x_refs, w_refs = refs[:n_x], refs[n_x:2 * n_x]
    aux, o_ref = refs[2 * n_x:-1], refs[-1]
    acc = jnp.dot(x_refs[0][...], w_refs[0][...], preferred_element_type=F32)
    for x_ref, w_ref in zip(x_refs[1:], w_refs[1:]):
        acc = acc + jnp.dot(x_ref[...], w_ref[...], preferred_element_type=F32)
    tn = acc.shape[1]
    if mode == "plain":
        o_ref[...] = acc
    elif mode == "residual":
        o_ref[...] = aux[0][...] + acc
    elif mode == "logsig":
        z = acc + aux[0][...]
        o_ref[...] = jnp.minimum(z, 0.0) - jnp.log1p(jnp.exp(-jnp.abs(z)))
    else:
        g = aux[0][...]
        for h in range(tn // HEAD_DIM):
            sl = slice(h * HEAD_DIM, (h + 1) * HEAD_DIM)
            y = _head_rmsnorm(acc[:, sl], g)
            if mode == "headnorm_rope":
                y = y * aux[1][...] + pltpu.roll(y, HEAD_DIM // 2, 1) * aux[2][...]
            o_ref[:, sl] = y


def proj(xs, ws, mode, aux=(), aux_kinds=(), tm=512, tn=512):
    n = xs[0].shape[0]
    ncols = ws[0].shape[1]
    tm, tn = min(tm, n), min(tn, ncols)
    in_specs = [pl.BlockSpec((tm, x.shape[1]), lambda j, i: (i, 0)) for x in xs]
    in_specs += [pl.BlockSpec((w.shape[0], tn), lambda j, i: (0, j)) for w in ws]
    for kind in aux_kinds:
        if kind == "col":
            in_specs.append(pl.BlockSpec((1, HEAD_DIM), lambda j, i: (0, 0)))
        elif kind == "row":
            in_specs.append(pl.BlockSpec((tm, HEAD_DIM), lambda j, i: (i, 0)))
        elif kind == "bias":
            in_specs.append(pl.BlockSpec((1, tn), lambda j, i: (0, j)))
        else:
            in_specs.append(pl.BlockSpec((tm, tn), lambda j, i: (i, j)))
    return pl.pallas_call(
        functools.partial(_proj_kernel, mode=mode, n_x=len(xs)),
        grid=(ncols // tn, n // tm),
        in_specs=in_specs,
        out_specs=pl.BlockSpec((tm, tn), lambda j, i: (i, j)),
        out_shape=jax.ShapeDtypeStruct((n, ncols), F32),
        compiler_params=_params(("parallel", "parallel")),
        name="proj_" + mode,
    )(*xs, *ws, *aux)


def _cumsum_kernel(lf_ref, c_ref, ct_ref, *, blk):
    s = lf_ref.shape[0]
    r = lax.broadcasted_iota(jnp.int32, (blk, blk), 0)
    c = lax.broadcasted_iota(jnp.int32, (blk, blk), 1)
    tri = jnp.where(r >= c, 1.0, 0.0).astype(F32)
    carry = jnp.zeros((1, LANES), F32)
    for b in range(s // blk):
        x = lf_ref[b * blk:(b + 1) * blk, :]
        cs = jnp.dot(tri, x, preferred_element_type=F32, precision=lax.Precision.HIGHEST) + carry
        c_ref[b * blk:(b + 1) * blk, :] = cs
        ct_ref[:, b * blk:(b + 1) * blk] = cs.T
        carry = cs[blk - 1:blk, :]


def cumsum_logf(lf, batch, seq):
    return pl.pallas_call(
        functools.partial(_cumsum_kernel, blk=LANES),
        grid=(batch,),
        in_specs=[pl.BlockSpec((seq, LANES), lambda b: (b, 0))],
        out_specs=[pl.BlockSpec((seq, LANES), lambda b: (b, 0)),
                   pl.BlockSpec((LANES, seq), lambda b: (b, 0))],
        out_shape=[jax.ShapeDtypeStruct((batch * seq, LANES), F32),
                   jax.ShapeDtypeStruct((batch * LANES, seq), F32)],
        compiler_params=_params(("parallel",)),
        name="cumsum_logf",
    )(lf)


def _flash_rows(q_bf, k_scr, v_scr, n_kv, q_pos0, tk, bias):
    tq = q_bf.shape[0]
    dv = v_scr.shape[1]
    row = q_pos0 + lax.broadcasted_iota(jnp.int32, (tq, tk), 0)
    col0 = lax.broadcasted_iota(jnp.int32, (tq, tk), 1)

    def body(kj, carry):
        m, l, acc = carry
        start = pl.multiple_of(kj * tk, tk)
        s = lax.dot_general(q_bf, k_scr[pl.ds(start, tk), :], _NT, preferred_element_type=F32) * ATTN_SCALE
        if bias is not None:
            s = s + bias[0] - bias[1][kj][0:1, :]
        s = jnp.where(col0 + kj * tk <= row, s, NEG_INF)
        m_new = jnp.maximum(m, jnp.max(s, axis=-1, keepdims=True))
        alpha = jnp.exp(m - m_new)
        p = jnp.exp(s - m_new)
        l = alpha * l + jnp.sum(p, axis=-1, keepdims=True)
        acc = alpha * acc + jnp.dot(p.astype(BF16), v_scr[pl.ds(start, tk), :], preferred_element_type=F32)
        return m_new, l, acc

    init = (jnp.full((tq, 1), NEG_INF, F32), jnp.zeros((tq, 1), F32), jnp.zeros((tq, dv), F32))
    _, l, acc = lax.fori_loop(0, n_kv, body, init)
    return acc / l


def _fox_prompt_kernel(q_ref, k_ref, v_ref, c_ref, ct_ref, o_ref, k_scr, v_scr, ck_scr, *, tq, tk):
    h = pl.program_id(1)
    s = q_ref.shape[0]
    k_scr[...] = k_ref[...].astype(BF16)
    v_scr[...] = v_ref[...].astype(BF16)
    lane = lax.broadcasted_iota(jnp.int32, (s, LANES), 1)
    cq = jnp.sum(jnp.where(lane == h, c_ref[...], 0.0), axis=-1, keepdims=True)
    ck_row = ct_ref[pl.ds(h, 1), :]
    for j in range(s // tk):
        ck_scr[j] = jnp.broadcast_to(ck_row[:, j * tk:(j + 1) * tk], (SUBLANES, tk))
    for qi in range(s // tq):
        rows = slice(qi * tq, (qi + 1) * tq)
        n_kv = (qi * tq + tq + tk - 1) // tk
        o = _flash_rows(q_ref[rows, :].astype(BF16), k_scr, v_scr, n_kv, qi * tq, tk, (cq[rows], ck_scr))
        o_ref[rows, :] = o.astype(o_ref.dtype)


def fox_prompt_attention(qf, kf, vf, c, ct, batch, seq, heads, tq=256, tk=256):
    blk = pl.BlockSpec((seq, HEAD_DIM), lambda b, h: (b, h))
    return pl.pallas_call(
        functools.partial(_fox_prompt_kernel, tq=tq, tk=tk),
        grid=(batch, heads),
        in_specs=[blk, blk, blk,
                  pl.BlockSpec((seq, LANES), lambda b, h: (b, 0)),
                  pl.BlockSpec((LANES, seq), lambda b, h: (b, 0))],
        out_specs=blk,
        out_shape=jax.ShapeDtypeStruct((batch * seq, heads * HEAD_DIM), BF16),
        scratch_shapes=[pltpu.VMEM((seq, HEAD_DIM), BF16), pltpu.VMEM((seq, HEAD_DIM), BF16),
                        pltpu.VMEM((seq // tk, SUBLANES, tk), F32)],
        compiler_params=_params(("parallel", "parallel")),
        name="fox_prompt_attention",
    )(qf, kf, vf, c, ct)


def _lambda_value(lq1, lk1, lq2, lk2, lam_init):
    a = jnp.sum(lq1[...] * lk1[...], axis=-1, keepdims=True)
    b = jnp.sum(lq2[...] * lk2[...], axis=-1, keepdims=True)
    return jnp.exp(a) - jnp.exp(b) + lam_init


def _sub_layer_norm(od, g, lam_init):
    ms = jnp.mean(od * od, axis=-1, keepdims=True)
    return (od * lax.rsqrt(ms + SUBLN_EPS) * g) * (1.0 - lam_init)


def _diff_prompt_kernel(q_ref, k_ref, v_ref, lq1, lk1, lq2, lk2, g_ref, o_ref, k_scr, v_scr, *, tq, tk, lam_init):
    s = q_ref.shape[0]
    lam = _lambda_value(lq1, lk1, lq2, lk2, lam_init)
    k_scr[0] = k_ref[:, :HEAD_DIM].astype(BF16)
    k_scr[1] = k_ref[:, HEAD_DIM:].astype(BF16)
    v_scr[...] = v_ref[...].astype(BF16)
    for qi in range(s // tq):
        rows = slice(qi * tq, (qi + 1) * tq)
        n_kv = (qi * tq + tq + tk - 1) // tk
        o1 = _flash_rows(q_ref[rows, :HEAD_DIM].astype(BF16), k_scr.at[0], v_scr, n_kv, qi * tq, tk, None)
        o2 = _flash_rows(q_ref[rows, HEAD_DIM:].astype(BF16), k_scr.at[1], v_scr, n_kv, qi * tq, tk, None)
        o_ref[rows, :] = _sub_layer_norm(o1 - lam * o2, g_ref[...], lam_init).astype(o_ref.dtype)


def diff_prompt_attention(qd, kd, vd, lams, g_sub, lam_init, batch, seq, heads, tq=256, tk=256):
    blk = pl.BlockSpec((seq, 2 * HEAD_DIM), lambda b, h: (b, h))
    vec = pl.BlockSpec((1, HEAD_DIM), lambda b, h: (0, 0))
    return pl.pallas_call(
        functools.partial(_diff_prompt_kernel, tq=tq, tk=tk, lam_init=lam_init),
        grid=(batch, heads),
        in_specs=[blk, blk, blk, vec, vec, vec, vec,
                  pl.BlockSpec((1, 2 * HEAD_DIM), lambda b, h: (0, 0))],
        out_specs=blk,
        out_shape=jax.ShapeDtypeStruct((batch * seq, heads * 2 * HEAD_DIM), BF16),
        scratch_shapes=[pltpu.VMEM((2, seq, HEAD_DIM), BF16), pltpu.VMEM((seq, 2 * HEAD_DIM), BF16)],
        compiler_params=_params(("parallel", "parallel")),
        name="diff_prompt_attention",
    )(qd, kd, vd, *lams, g_sub)


def _online_update(s, mask, v_bf, m_scr, l_scr, acc_scr):
    s = jnp.where(mask, s, NEG_INF)
    m_prev = m_scr[...]
    m_new = jnp.maximum(m_prev, jnp.max(s, axis=-1, keepdims=True))
    alpha = jnp.exp(m_prev - m_new)
    p = jnp.exp(s - m_new)
    l_scr[...] = alpha * l_scr[...] + jnp.sum(p, axis=-1, keepdims=True)
    acc_scr[...] = alpha * acc_scr[...] + jnp.dot(p.astype(BF16), v_bf, preferred_element_type=F32)
    m_scr[...] = m_new


def _split3(x, lane):
    hi = x.astype(BF16).astype(F32)
    r1 = x - hi
    mid = r1.astype(BF16).astype(F32)
    lo = r1 - mid
    n = LANES // SUBLANES
    return jnp.where(lane < n, hi, jnp.where(lane < 2 * n, mid, jnp.where(lane < 3 * n, lo, 0.0)))


def _rows_by_piece(x, width, order):
    return jnp.concatenate([x[:, j * width:(j + 1) * width] for j in order], axis=0)


def _fox_decode_kernel(pt_ref, q_ref, kn_ref, vn_ref, lfn_ref, *rest, heads, group, tnew):
    del pt_ref
    page_refs, o_ref = rest[:3 * group], rest[3 * group]
    q_scr, m_scr, l_scr, acc_scr, later_scr = rest[3 * group + 1:]
    j = pl.program_id(1)
    nrow = heads * tnew
    row = lax.broadcasted_iota(jnp.int32, (nrow, LANES), 0)
    lane = lax.broadcasted_iota(jnp.int32, (nrow, LANES), 1)
    order = range(heads)

    @pl.when(j == 0)
    def _():
        for g in range(group):
            tok = slice(g * tnew, (g + 1) * tnew)
            qm = _rows_by_piece(q_ref[tok, :], HEAD_DIM, order) * ATTN_SCALE
            onehot = jnp.where((lane < LOGF_COPIES * heads) & ((lane & (heads - 1)) == (row >> 3)), 1.0, 0.0)
            q_scr[g] = jnp.concatenate([qm, onehot], axis=1).astype(BF16)
            m_scr[g] = jnp.full(m_scr.shape[1:], NEG_INF, F32)
            l_scr[g] = jnp.zeros(l_scr.shape[1:], F32)
            acc_scr[g] = jnp.zeros(acc_scr.shape[1:], F32)
            later_scr[g] = jnp.zeros(later_scr.shape[1:], F32)
            lfn = jnp.concatenate([lfn_ref[tok, :], jnp.zeros((LANES - tnew, LANES), F32)], axis=0)
            tri = jnp.where(row >= lane, 1.0, 0.0).astype(F32)
            cn = jnp.dot(tri, lfn, preferred_element_type=F32, precision=lax.Precision.HIGHEST)[0:tnew]
            bn = _split3(-cn, lane[0:tnew])
            kn = _rows_by_piece(kn_ref[tok, :], HEAD_DIM, order)
            kaug = jnp.concatenate([kn, jnp.concatenate([bn] * heads, axis=0)], axis=1).astype(BF16)
            s = lax.dot_general(q_scr[g], kaug, _NT, preferred_element_type=F32)
            mask = ((lane >> 3) == (row >> 3)) & ((lane & 7) <= (row & 7))
            _online_update(s, mask, _rows_by_piece(vn_ref[tok, :], HEAD_DIM, order).astype(BF16),
                           m_scr.at[g], l_scr.at[g], acc_scr.at[g])

    for g in range(group):
        kp_ref, vp_ref, lfp_ref = page_refs[3 * g:3 * g + 3]
        lf = lfp_ref[0]
        triu = jnp.where(row < lane, 1.0, 0.0).astype(F32)
        suf = jnp.dot(triu, lf, preferred_element_type=F32, precision=lax.Precision.HIGHEST) + later_scr[g]
        later_scr[g] = later_scr[g] + jnp.sum(lf, axis=0, keepdims=True)
        bk = _split3(suf, lane)
        page = bk.shape[0]
        bk_rows = jnp.broadcast_to(bk[:, None, :], (page, heads, LANES)).reshape(page * heads, LANES)
        kaug = jnp.concatenate([kp_ref[0].astype(BF16), bk_rows.astype(BF16)], axis=1)
        s = lax.dot_general(q_scr[g], kaug, _NT, preferred_element_type=F32)
        col = lax.broadcasted_iota(jnp.int32, s.shape, 1)
        rw = lax.broadcasted_iota(jnp.int32, s.shape, 0)
        _online_update(s, (col & (heads - 1)) == (rw >> 3), vp_ref[0].astype(BF16),
                       m_scr.at[g], l_scr.at[g], acc_scr.at[g])

    @pl.when(j == pl.num_programs(1) - 1)
    def _():
        for g in range(group):
            o = acc_scr[g] / l_scr[g]
            o_ref[g * tnew:(g + 1) * tnew, :] = jnp.concatenate(
                [o[h * tnew:(h + 1) * tnew] for h in range(heads)], axis=1)


def fox_decode_attention(page_table, qf, kf, vf, lfn, cache_k, cache_v, cache_lf3, dec_batch, tnew, heads):
    n_pages = page_table.shape[1]
    width = heads * HEAD_DIM
    rows_kv = cache_k.shape[1]
    page = cache_lf3.shape[1]
    group = DECODE_GROUP if dec_batch % DECODE_GROUP == 0 else 1
    tok = pl.BlockSpec((group * tnew, width), lambda b, j, pt: (b, 0))

    def pg(g):
        return lambda b, j, pt: (pt[(b * group + g) * n_pages + n_pages - 1 - j], 0, 0)

    page_specs = []
    for g in range(group):
        page_specs += [pl.BlockSpec((1, rows_kv, HEAD_DIM), pg(g)), pl.BlockSpec((1, rows_kv, HEAD_DIM), pg(g)),
                       pl.BlockSpec((1, page, LANES), pg(g))]
    nrow = heads * tnew
    grid_spec = pltpu.PrefetchScalarGridSpec(
        num_scalar_prefetch=1,
        grid=(dec_batch // group, n_pages),
        in_specs=[tok, tok, tok, pl.BlockSpec((group * tnew, LANES), lambda b, j, pt: (b, 0))] + page_specs,
        out_specs=tok,
        scratch_shapes=[pltpu.VMEM((group, nrow, 2 * HEAD_DIM), BF16),
                        pltpu.VMEM((group, nrow, 1), F32), pltpu.VMEM((group, nrow, 1), F32),
                        pltpu.VMEM((group, nrow, HEAD_DIM), F32), pltpu.VMEM((group, 1, LANES), F32)])
    return pl.pallas_call(
        functools.partial(_fox_decode_kernel, heads=heads, group=group, tnew=tnew),
        grid_spec=grid_spec,
        out_shape=jax.ShapeDtypeStruct((dec_batch * tnew, width), F32),
        compiler_params=_params(("parallel", "arbitrary")),
        name="fox_decode_attention",
    )(page_table.reshape(-1), qf, kf, vf, lfn, *([cache_k, cache_v, cache_lf3] * group))


def _diff_decode_kernel(pt_ref, q_ref, kn_ref, vn_ref, lq1, lk1, lq2, lk2, g_ref, *rest, heads, lam_init, group, tnew):
    del pt_ref
    page_refs, o_ref = rest[:2 * group], rest[2 * group]
    q_scr, m_scr, l_scr, acc_scr = rest[2 * group + 1:]
    j = pl.program_id(1)
    half = heads * tnew
    nrow = 2 * half
    order = [2 * h for h in range(heads)] + [2 * h + 1 for h in range(heads)]

    @pl.when(j == 0)
    def _():
        row = lax.broadcasted_iota(jnp.int32, (nrow, nrow), 0)
        lane = lax.broadcasted_iota(jnp.int32, (nrow, nrow), 1)
        for g in range(group):
            tok = slice(g * tnew, (g + 1) * tnew)
            q_scr[g] = (_rows_by_piece(q_ref[tok, :], HEAD_DIM, order) * ATTN_SCALE).astype(BF16)
            m_scr[g] = jnp.full(m_scr.shape[1:], NEG_INF, F32)
            l_scr[g] = jnp.zeros(l_scr.shape[1:], F32)
            acc_scr[g] = jnp.zeros(acc_scr.shape[1:], F32)
            kn = _rows_by_piece(kn_ref[tok, :], HEAD_DIM, order).astype(BF16)
            s = lax.dot_general(q_scr[g], kn, _NT, preferred_element_type=F32)
            mask = ((lane >> 3) == (row >> 3)) & ((lane & 7) <= (row & 7))
            vn = _rows_by_piece(vn_ref[tok, :], 2 * HEAD_DIM, [jj // 2 for jj in order]).astype(BF16)
            _online_update(s, mask, vn, m_scr.at[g], l_scr.at[g], acc_scr.at[g])

    for g in range(group):
        kp_ref, vp_ref = page_refs[2 * g:2 * g + 2]
        n_keys = vp_ref.shape[1]
        s = jnp.concatenate(
            [lax.dot_general(q_scr[g, c * half:(c + 1) * half, :],
                             kp_ref[0, pl.ds(c, n_keys, stride=2), :].astype(BF16), _NT, preferred_element_type=F32)
             for c in range(2)], axis=0)
        col = lax.broadcasted_iota(jnp.int32, s.shape, 1)
        rw = lax.broadcasted_iota(jnp.int32, s.shape, 0)
        _online_update(s, (col & (heads - 1)) == ((rw & (half - 1)) >> 3), vp_ref[0].astype(BF16),
                       m_scr.at[g], l_scr.at[g], acc_scr.at[g])

    @pl.when(j == pl.num_programs(1) - 1)
    def _():
        lam = _lambda_value(lq1, lk1, lq2, lk2, lam_init)
        for g in range(group):
            o = acc_scr[g] / l_scr[g]
            y = _sub_layer_norm(o[:half] - lam * o[half:], g_ref[...], lam_init)
            o_ref[g * tnew:(g + 1) * tnew, :] = jnp.concatenate(
                [y[h * tnew:(h + 1) * tnew] for h in range(heads)], axis=1)


def diff_decode_attention(page_table, qd, kd, vd, lams, g_sub, lam_init, cache_k, cache_v, dec_batch, tnew, heads):
    n_pages = page_table.shape[1]
    width = heads * 2 * HEAD_DIM
    group = DECODE_GROUP if dec_batch % DECODE_GROUP == 0 else 1
    tok = pl.BlockSpec((group * tnew, width), lambda b, j, pt: (b, 0))
    vec = pl.BlockSpec((1, HEAD_DIM), lambda b, j, pt: (0, 0))

    def pg(g):
        return lambda b, j, pt: (pt[(b * group + g) * n_pages + j], 0, 0)

    page_specs = []
    for g in range(group):
        page_specs += [pl.BlockSpec((1,) + cache_k.shape[1:], pg(g)), pl.BlockSpec((1,) + cache_v.shape[1:], pg(g))]
    nrow = 2 * heads * tnew
    grid_spec = pltpu.PrefetchScalarGridSpec(
        num_scalar_prefetch=1,
        grid=(dec_batch // group, n_pages),
        in_specs=[tok, tok, tok, vec, vec, vec, vec,
                  pl.BlockSpec((1, 2 * HEAD_DIM), lambda b, j, pt: (0, 0))] + page_specs,
        out_specs=tok,
        scratch_shapes=[pltpu.VMEM((group, nrow, HEAD_DIM), BF16),
                        pltpu.VMEM((group, nrow, 1), F32), pltpu.VMEM((group, nrow, 1), F32),
                        pltpu.VMEM((group, nrow, 2 * HEAD_DIM), F32)])
    return pl.pallas_call(
        functools.partial(_diff_decode_kernel, heads=heads, lam_init=lam_init, group=group, tnew=tnew),
        grid_spec=grid_spec,
        out_shape=jax.ShapeDtypeStruct((dec_batch * tnew, width), F32),
        compiler_params=_params(("parallel", "arbitrary")),
        name="diff_decode_attention",
    )(page_table.reshape(-1), qd, kd, vd, *lams, g_sub, *([cache_k, cache_v] * group))


def _topk_rows(s, payload=None):
    n, t = s.shape
    rid = lax.broadcasted_iota(jnp.int32, (n, t), 0).astype(F32)
    slot = lax.broadcasted_iota(jnp.int32, (PEER_TOPK, t), 0)
    vals = jnp.zeros((PEER_TOPK, t), F32)
    idxs = jnp.zeros((PEER_TOPK, t), F32)
    pays = jnp.zeros((PEER_TOPK, t), F32)
    for r in range(PEER_TOPK):
        m = jnp.max(s, axis=0, keepdims=True)
        idx = jnp.min(jnp.where(s == m, rid, float(n)), axis=0, keepdims=True)
        hit = rid == idx
        vals = jnp.where(slot == r, m, vals)
        idxs = jnp.where(slot == r, idx, idxs)
        if payload is not None:
            pays = jnp.where(slot == r, jnp.sum(jnp.where(hit, payload, 0.0), axis=0, keepdims=True), pays)
        s = jnp.where(hit, -jnp.inf, s)
    return vals, idxs, pays


def _router_kernel(h_ref, wq_ref, ka_ref, kb_ref, g_ref, e_ref):
    t = h_ref.shape[0]
    q = jnp.dot(h_ref[...].astype(BF16), wq_ref[...], preferred_element_type=F32)
    ka = ka_ref[...].astype(BF16)
    kb = kb_ref[...].astype(BF16)
    half = ka.shape[1]
    for h in range(PEER_HEADS):
        qa = q[:, (2 * h) * half:(2 * h + 1) * half].astype(BF16)
        qb = q[:, (2 * h + 1) * half:(2 * h + 2) * half].astype(BF16)
        sa = lax.dot_general(ka, qa, _NT, preferred_element_type=F32)
        sb = lax.dot_general(kb, qb, _NT, preferred_element_type=F32)
        va, ia, _ = _topk_rows(sa)
        vb, ib, _ = _topk_rows(sb)
        rep = lambda x: jnp.concatenate(
            [jnp.broadcast_to(x[i:i + 1, :], (PEER_TOPK, t)) for i in range(PEER_TOPK)], axis=0)
        tile = lambda x: jnp.concatenate([x] * PEER_TOPK, axis=0)
        cand = rep(va) + tile(vb)
        cidx = rep(ia) * float(N_KEYS) + tile(ib)
        sv, _, e = _topk_rows(cand, cidx)
        ex = jnp.exp(sv - sv[0:1, :])
        g_ref[h * PEER_TOPK:(h + 1) * PEER_TOPK, :] = ex / jnp.sum(ex, axis=0, keepdims=True)
        e_ref[h * PEER_TOPK:(h + 1) * PEER_TOPK, :] = e.astype(jnp.int32)


def peer_router(h, wq, keys_a, keys_b, tm=256):
    n, d = h.shape
    tm = min(tm, n)
    return pl.pallas_call(
        _router_kernel,
        grid=(n // tm,),
        in_specs=[pl.BlockSpec((tm, d), lambda i: (i, 0)),
                  pl.BlockSpec(wq.shape, lambda i: (0, 0)),
                  pl.BlockSpec(keys_a.shape, lambda i: (0, 0)),
                  pl.BlockSpec(keys_b.shape, lambda i: (0, 0))],
        out_specs=[pl.BlockSpec((PEER_PAIRS, tm), lambda i: (0, i)),
                   pl.BlockSpec((PEER_PAIRS, tm), lambda i: (0, i))],
        out_shape=[jax.ShapeDtypeStruct((PEER_PAIRS, n), F32),
                   jax.ShapeDtypeStruct((PEER_PAIRS, n), jnp.int32)],
        compiler_params=_params(("parallel",)),
        name="peer_router",
    )(h, wq, keys_a, keys_b)


def pack_bf16_halves(t):
    half = t.shape[1] // 2
    b = lax.bitcast_convert_type(t.astype(BF16), jnp.uint16).astype(jnp.uint32)
    return lax.bitcast_convert_type((b[:, half:] << 16) | b[:, :half], jnp.int32)


def _unpack_bf16_halves(words):
    lo = lax.bitcast_convert_type(words << 16, F32)
    hi = lax.bitcast_convert_type(words & jnp.int32(-65536), F32)
    return lo, hi


PEER_SLOTS = 4
PEER_AHEAD = 2


def _peer_expert_kernel(idx_ref, h_ref, x_ref, g_ref, u_hbm, v_hbm, o_ref, *scratch):
    ubufs, vbufs = scratch[:PEER_SLOTS], scratch[PEER_SLOTS:2 * PEER_SLOTS]
    sem, xb_scr, part_scr = scratch[2 * PEER_SLOTS:]
    tb, d = h_ref.shape
    half = d // 2
    n_chunks = half // LANES
    groups = PEER_PAIRS // SUBLANES
    per_chunk = PEER_PAIRS // (2 * n_chunks)

    def start_rows(t, slot, ks):
        for k in ks:
            e = idx_ref[t, k]
            pltpu.make_async_copy(u_hbm.at[pl.ds(e, 1), :], ubufs[slot].at[pl.ds(k, 1), :], sem.at[0, slot]).start()
            pltpu.make_async_copy(v_hbm.at[pl.ds(e, 1), :], vbufs[slot].at[pl.ds(k, 1), :], sem.at[1, slot]).start()

    def wait(slot):
        pltpu.make_async_copy(u_hbm.at[pl.ds(0, PEER_PAIRS), :], ubufs[slot], sem.at[0, slot]).wait()
        pltpu.make_async_copy(v_hbm.at[pl.ds(0, PEER_PAIRS), :], vbufs[slot], sem.at[1, slot]).wait()

    lane = lax.broadcasted_iota(jnp.int32, (PEER_PAIRS, tb), 1)
    for a in range(PEER_AHEAD):
        start_rows(a, a, range(PEER_PAIRS))

    def token(t, slot):
        t_next = jnp.minimum(t + PEER_AHEAD, tb - 1)
        s_next = (slot + PEER_AHEAD) % PEER_SLOTS
        wait(slot)
        xb_scr[...] = jnp.broadcast_to(h_ref[pl.ds(t, 1), :], (SUBLANES, d))
        acc = jnp.zeros((groups, SUBLANES, LANES), F32)
        for c in range(n_chunks):
            start_rows(t_next, s_next, range(c * per_chunk, (c + 1) * per_chunk))
            sl = slice(c * LANES, (c + 1) * LANES)
            sh = slice(half + c * LANES, half + (c + 1) * LANES)
            lo, hi = _unpack_bf16_halves(ubufs[slot][:, sl].reshape(groups, SUBLANES, LANES))
            acc = acc + lo * xb_scr[:, sl][None] + hi * xb_scr[:, sh][None]
        act = jnp.sum(acc.reshape(PEER_PAIRS, LANES), axis=-1, keepdims=True)
        gate = jnp.sum(jnp.where(lane == t, g_ref[...], 0.0), axis=-1, keepdims=True)
        w = gate * (0.5 * act * (1.0 + lax.erf(act * math.sqrt(0.5))))
        wb = jnp.broadcast_to(w, (PEER_PAIRS, LANES)).reshape(groups, SUBLANES, LANES)
        for c in range(n_chunks):
            start_rows(t_next, s_next, range((n_chunks + c) * per_chunk, (n_chunks + c + 1) * per_chunk))
            sl = slice(c * LANES, (c + 1) * LANES)
            sh = slice(half + c * LANES, half + (c + 1) * LANES)
            lo, hi = _unpack_bf16_halves(vbufs[slot][:, sl].reshape(groups, SUBLANES, LANES))
            part_scr[:, sl] = jnp.sum(lo * wb, axis=0)
            part_scr[:, sh] = jnp.sum(hi * wb, axis=0)
        o_ref[pl.ds(t, 1), :] = x_ref[pl.ds(t, 1), :] + jnp.sum(part_scr[...], axis=0, keepdims=True)

    def body(i, _):
        for slot in range(PEER_SLOTS):
            token(i * PEER_SLOTS + slot, slot)
        return 0

    lax.fori_loop(0, tb // PEER_SLOTS, body, 0)
    for a in range(PEER_AHEAD):
        wait((tb + a) % PEER_SLOTS)


def peer_experts(eidx, h, x, gates, u_packed, v_packed, tb=128):
    n, d = h.shape
    tb = min(tb, n)
    assert tb % PEER_SLOTS == 0 and n % tb == 0 and PEER_AHEAD < PEER_SLOTS
    tok = pl.BlockSpec((tb, d), lambda i: (i, 0))
    buf = pltpu.VMEM((PEER_PAIRS, d // 2), jnp.int32)
    return pl.pallas_call(
        _peer_expert_kernel,
        grid=(n // tb,),
        in_specs=[pl.BlockSpec((tb, PEER_PAIRS), lambda i: (i, 0), memory_space=pltpu.SMEM),
                  tok, tok,
                  pl.BlockSpec((PEER_PAIRS, tb), lambda i: (0, i)),
                  pl.BlockSpec(memory_space=pl.ANY),
                  pl.BlockSpec(memory_space=pl.ANY)],
        out_specs=tok,
        out_shape=jax.ShapeDtypeStruct((n, d), F32),
        scratch_shapes=[buf] * (2 * PEER_SLOTS) + [
            pltpu.SemaphoreType.DMA((2, PEER_SLOTS)),
            pltpu.VMEM((SUBLANES, d), F32), pltpu.VMEM((SUBLANES, d), F32)],
        compiler_params=_params(("arbitrary",)),
        name="peer_experts",
    )(eidx, h, x, gates, u_packed, v_packed)


def _rope_tables(pos):
    inv = ROPE_THETA ** (-jnp.arange(0, HEAD_DIM, 2, dtype=F32) / HEAD_DIM)
    ang = pos.astype(F32)[:, None] * inv[None, :]
    cos, sin = jnp.cos(ang), jnp.sin(ang)
    return jnp.concatenate([cos, cos], axis=-1), jnp.concatenate([-sin, sin], axis=-1)


def _mixer_inputs(x, pos, w, tm):
    h = rmsnorm(x, w["g_attn"], BF16, tm=min(tm, 256))
    cos, sin = _rope_tables(pos)
    hn = functools.partial(proj, [h], mode="headnorm", aux_kinds=("col",), tm=tm)
    hr = functools.partial(proj, [h], mode="headnorm_rope", aux_kinds=("col", "row", "row"), tm=tm)
    qf = hn([w["w_fq"]], aux=(w["g_qf"],))
    kf = hn([w["w_fk"]], aux=(w["g_kf"],))
    vf = proj([h], [w["w_fv"]], "plain", tm=tm)
    lf3 = proj([h], [w["w_ff3"]], "logsig", aux=(w["b_ff3"],), aux_kinds=("bias",), tm=tm)
    qd = hr([w["w_dq"]], aux=(w["g_qd"], cos, sin))
    kd = hr([w["w_dk"]], aux=(w["g_kd"], cos, sin))
    vd = proj([h], [w["w_dv"]], "plain", tm=tm)
    return qf, kf, vf, lf3, qd, kd, vd


def _merge_and_ffn(x, of, od, w, tm):
    x1 = proj([of, od], [w["w_out_f"], w["w_out_d"]], "residual", aux=(x,), aux_kinds=("full",), tm=tm)
    h2 = rmsnorm(x1, w["g_ffn"], F32, tm=min(tm, 256))
    gates, eidx_t = peer_router(h2, w["w_peer_q"], w["keys_a"], w["keys_b"])
    return peer_experts(eidx_t.T, h2, x1, gates, w["peer_u"], w["peer_v"])


def kernel(x_prompt, x_sample, cache_fox_k, cache_fox_v, cache_fox_logf, cache_diff_k, cache_diff_v, page_table, g_attn_norm, w_in, b_forget, g_q_fox, g_k_fox, g_q_diff, g_k_diff, lambda_q1, lambda_k1, lambda_q2, lambda_k2, g_diff_sub, w_out, g_ffn_norm, w_peer_q, peer_keys_a, peer_keys_b, peer_u, peer_v):
    batch, seq, d_model = x_prompt.shape
    dec_batch, dec_seq, _ = x_sample.shape
    depth, n_pool, page_size, fox_heads, _ = cache_fox_k.shape
    diff_heads = cache_diff_k.shape[3]
    fox_w = fox_heads * HEAD_DIM
    diff_w = diff_heads * 2 * HEAD_DIM
    past_len = page_table.shape[1] * page_size
    assert fox_heads * LOGF_COPIES <= LANES and dec_seq == SUBLANES

    xp = x_prompt.reshape(batch * seq, d_model)
    xs = x_sample.reshape(dec_batch * dec_seq, d_model)
    pos_p = jnp.tile(jnp.arange(seq, dtype=jnp.int32), batch)
    pos_s = jnp.tile(past_len + jnp.arange(dec_seq, dtype=jnp.int32), dec_batch)
    outs_p, outs_s = [], []
    for l in range(depth):
        lam_init = 0.8 - 0.6 * math.exp(-0.3 * l)
        wi = w_in[l]
        o = 0
        sec = {}
        for name, width in (("w_fq", fox_w), ("w_fk", fox_w), ("w_fv", fox_w), ("w_ff", fox_heads),
                            ("w_dq", diff_w), ("w_dk", diff_w), ("w_dv", diff_w)):
            sec[name] = wi[:, o:o + width]
            o += width
        w_ff = sec.pop("w_ff")
        pad = LANES - LOGF_COPIES * fox_heads
        w = {k: v.astype(BF16) for k, v in sec.items()}
        w["w_ff3"] = jnp.pad(jnp.tile(w_ff, (1, LOGF_COPIES)), ((0, 0), (0, pad))).astype(BF16)
        w["b_ff3"] = jnp.pad(jnp.tile(b_forget[l], LOGF_COPIES), (0, pad)).reshape(1, LANES)
        w["g_attn"], w["g_ffn"] = g_attn_norm[l], g_ffn_norm[l]
        for name, g in (("g_qf", g_q_fox), ("g_kf", g_k_fox), ("g_qd", g_q_diff), ("g_kd", g_k_diff)):
            w[name] = g[l].reshape(1, HEAD_DIM)
        w["w_out_f"] = w_out[l, :fox_w].astype(BF16)
        w["w_out_d"] = w_out[l, fox_w:].astype(BF16)
        w["w_peer_q"] = w_peer_q[l].astype(BF16)
        w["keys_a"], w["keys_b"] = peer_keys_a[l], peer_keys_b[l]
        w["peer_u"], w["peer_v"] = pack_bf16_halves(peer_u[l]), pack_bf16_halves(peer_v[l])
        lams = tuple(a[l].reshape(1, HEAD_DIM) for a in (lambda_q1, lambda_k1, lambda_q2, lambda_k2))
        g_sub = g_diff_sub[l].reshape(1, 2 * HEAD_DIM)

        qf, kf, vf, lf3, qd, kd, vd = _mixer_inputs(xp, pos_p, w, tm=512)
        c, ct = cumsum_logf(lf3, batch, seq)
        of = fox_prompt_attention(qf, kf, vf, c, ct, batch, seq, fox_heads)
        od = diff_prompt_attention(qd, kd, vd, lams, g_sub, lam_init, batch, seq, diff_heads)
        xp = _merge_and_ffn(xp, of, od, w, tm=512)
        outs_p.append((kf, vf, lf3[:, :fox_heads], kd, vd))

        qf, kf, vf, lf3, qd, kd, vd = _mixer_inputs(xs, pos_s, w, tm=256)
        lf_cache = cache_fox_logf[l].astype(F32)
        lf_cache3 = jnp.pad(jnp.tile(lf_cache, (1, 1, LOGF_COPIES)), ((0, 0), (0, 0), (0, pad)))
        of = fox_decode_attention(page_table, qf, kf, vf, lf3,
                                  cache_fox_k[l].reshape(n_pool, page_size * fox_heads, HEAD_DIM),
                                  cache_fox_v[l].reshape(n_pool, page_size * fox_heads, HEAD_DIM),
                                  lf_cache3, dec_batch, dec_seq, fox_heads)
        od = diff_decode_attention(page_table, qd, kd, vd, lams, g_sub, lam_init,
                                   cache_diff_k[l].reshape(n_pool, page_size * diff_heads * 2, HEAD_DIM),
                                   cache_diff_v[l].reshape(n_pool, page_size * diff_heads, 2 * HEAD_DIM),
                                   dec_batch, dec_seq, diff_heads)
        xs = _merge_and_ffn(xs, of.astype(BF16), od.astype(BF16), w, tm=256)
        outs_s.append((kf, vf, lf3[:, :fox_heads], kd, vd))

    def stack(outs, i, shape):
        return jnp.stack([o[i].reshape(shape) for o in outs])

    res = [xp.reshape(batch, seq, d_model), xs.reshape(dec_batch, dec_seq, d_model)]
    for outs, b, t in ((outs_p, batch, seq), (outs_s, dec_batch, dec_seq)):
        res += [stack(outs, 0, (b, t, fox_heads, HEAD_DIM)), stack(outs, 1, (b, t, fox_heads, HEAD_DIM)),
                stack(outs, 2, (b, t, fox_heads)), stack(outs, 3, (b, t, diff_heads, 2, HEAD_DIM)),
                stack(outs, 4, (b, t, diff_heads, 2 * HEAD_DIM))]
    return tuple(res)
```

```python
import functools
import math

import jax
import jax.numpy as jnp
from jax import lax
from jax.experimental import pallas as pl
from jax.experimental.pallas import tpu as pltpu

F32 = jnp.float32
BF16 = jnp.bfloat16

HEAD_DIM = 128
LANES = 128
SUBLANES = 8
ROPE_THETA = 10000.0
ATTN_SCALE = HEAD_DIM ** -0.5
NEG_INF = -1e30
NORM_EPS = 1e-6
SUBLN_EPS = 1e-5
N_KEYS = 128
PEER_HEADS = 8
PEER_TOPK = 16
PEER_PAIRS = PEER_HEADS * PEER_TOPK
LOGF_COPIES = 3
DECODE_GROUP = 2
VMEM_LIMIT = 56 * 1024 * 1024

_NT = (((1,), (1,)), ((), ()))


def _params(sem, vmem=VMEM_LIMIT):
    return pltpu.CompilerParams(dimension_semantics=sem, vmem_limit_bytes=vmem)


def _rmsnorm_kernel(x_ref, g_ref, o_ref, *, eps):
    x = x_ref[...]
    ms = jnp.mean(x * x, axis=-1, keepdims=True)
    o_ref[...] = (x * lax.rsqrt(ms + eps) * g_ref[...]).astype(o_ref.dtype)


def rmsnorm(x, g, out_dtype, tm=256):
    n, d = x.shape
    tm = min(tm, n)
    return pl.pallas_call(
        functools.partial(_rmsnorm_kernel, eps=NORM_EPS),
        grid=(n // tm,),
        in_specs=[pl.BlockSpec((tm, d), lambda i: (i, 0)),
                  pl.BlockSpec((1, d), lambda i: (0, 0))],
        out_specs=pl.BlockSpec((tm, d), lambda i: (i, 0)),
        out_shape=jax.ShapeDtypeStruct((n, d), out_dtype),
        compiler_params=_params(("parallel",)),
        name="rmsnorm",
    )(x, g.reshape(1, d))


def _head_rmsnorm(x, g):
    ms = jnp.mean(x * x, axis=-1, keepdims=True)
    return x * lax.rsqrt(ms + NORM_EPS) * g


def _proj_kernel(*refs, mode, n_x):
    x_refs, w_refs = refs[:n_x], refs[n_x:2 * n_x]
    aux, o_ref = refs[2 * n_x:-1], refs[-1]
    acc = jnp.dot(x_refs[0][...], w_refs[0][...], preferred_element_type=F32)
    for x_ref, w_ref in zip(x_refs[1:], w_refs[1:]):
        acc = acc + jnp.dot(x_ref[...], w_ref[...], preferred_element_type=F32)
    tn = acc.shape[1]
    if mode == "plain":
        o_ref[...] = acc
    elif mode == "residual":
        o_ref[...] = aux[0][...] + acc
    elif mode == "logsig":
        z = acc + aux[0][...]
        o_ref[...] = jnp.minimum(z, 0.0) - jnp.log1p(jnp.exp(-jnp.abs(z)))
    else:
        g = aux[0][...]
        for h in range(tn // HEAD_DIM):
            sl = slice(h * HEAD_DIM, (h + 1) * HEAD_DIM)
            y = _head_rmsnorm(acc[:, sl], g)
            if mode == "headnorm_rope":
                y = y * aux[1][...] + pltpu.roll(y, HEAD_DIM // 2, 1) * aux[2][...]
            o_ref[:, sl] = y


def proj(xs, ws, mode, aux=(), aux_kinds=(), tm=512, tn=512):
    n = xs[0].shape[0]
    ncols = ws[0].shape[1]
    tm, tn = min(tm, n), min(tn, ncols)
    in_specs = [pl.BlockSpec((tm, x.shape[1]), lambda j, i: (i, 0)) for x in xs]
    in_specs += [pl.BlockSpec((w.shape[0], tn), lambda j, i: (0, j)) for w in ws]
    for kind in aux_kinds:
        if kind == "col":
            in_specs.append(pl.BlockSpec((1, HEAD_DIM), lambda j, i: (0, 0)))
        elif kind == "row":
            in_specs.append(pl.BlockSpec((tm, HEAD_DIM), lambda j, i: (i, 0)))
        elif kind == "bias":
            in_specs.append(pl.BlockSpec((1, tn), lambda j, i: (0, j)))
        else:
            in_specs.append(pl.BlockSpec((tm, tn), lambda j, i: (i, j)))
    return pl.pallas_call(
        functools.partial(_proj_kernel, mode=mode, n_x=len(xs)),
        grid=(ncols // tn, n // tm),
        in_specs=in_specs,
        out_specs=pl.BlockSpec((tm, tn), lambda j, i: (i, j)),
        out_shape=jax.ShapeDtypeStruct((n, ncols), F32),
        compiler_params=_params(("parallel", "parallel")),
        name="proj_" + mode,
    )(*xs, *ws, *aux)


def _cumsum_kernel(lf_ref, c_ref, ct_ref, *, blk):
    s = lf_ref.shape[0]
    r = lax.broadcasted_iota(jnp.int32, (blk, blk), 0)
    c = lax.broadcasted_iota(jnp.int32, (blk, blk), 1)
    tri = jnp.where(r >= c, 1.0, 0.0).astype(F32)
    carry = jnp.zeros((1, LANES), F32)
    for b in range(s // blk):
        x = lf_ref[b * blk:(b + 1) * blk, :]
        cs = jnp.dot(tri, x, preferred_element_type=F32, precision=lax.Precision.HIGHEST) + carry
        c_ref[b * blk:(b + 1) * blk, :] = cs
        ct_ref[:, b * blk:(b + 1) * blk] = cs.T
        carry = cs[blk - 1:blk, :]


def cumsum_logf(lf, batch, seq):
    return pl.pallas_call(
        functools.partial(_cumsum_kernel, blk=LANES),
        grid=(batch,),
        in_specs=[pl.BlockSpec((seq, LANES), lambda b: (b, 0))],
        out_specs=[pl.BlockSpec((seq, LANES), lambda b: (b, 0)),
                   pl.BlockSpec((LANES, seq), lambda b: (b, 0))],
        out_shape=[jax.ShapeDtypeStruct((batch * seq, LANES), F32),
                   jax.ShapeDtypeStruct((batch * LANES, seq), F32)],
        compiler_params=_params(("parallel",)),
        name="cumsum_logf",
    )(lf)


def _flash_rows(q_bf, k_scr, v_scr, n_kv, q_pos0, tk, bias):
    tq = q_bf.shape[0]
    dv = v_scr.shape[1]
    row = q_pos0 + lax.broadcasted_iota(jnp.int32, (tq, tk), 0)
    col0 = lax.broadcasted_iota(jnp.int32, (tq, tk), 1)

    def body(kj, carry):
        m, l, acc = carry
        start = pl.multiple_of(kj * tk, tk)
        s = lax.dot_general(q_bf, k_scr[pl.ds(start, tk), :], _NT, preferred_element_type=F32) * ATTN_SCALE
        if bias is not None:
            s = s + bias[0] - bias[1][kj][0:1, :]
        s = jnp.where(col0 + kj * tk <= row, s, NEG_INF)
        m_new = jnp.maximum(m, jnp.max(s, axis=-1, keepdims=True))
        alpha = jnp.exp(m - m_new)
        p = jnp.exp(s - m_new)
        l = alpha * l + jnp.sum(p, axis=-1, keepdims=True)
        acc = alpha * acc + jnp.dot(p.astype(BF16), v_scr[pl.ds(start, tk), :], preferred_element_type=F32)
        return m_new, l, acc

    init = (jnp.full((tq, 1), NEG_INF, F32), jnp.zeros((tq, 1), F32), jnp.zeros((tq, dv), F32))
    _, l, acc = lax.fori_loop(0, n_kv, body, init)
    return acc / l


def _fox_prompt_kernel(q_ref, k_ref, v_ref, c_ref, ct_ref, o_ref, k_scr, v_scr, ck_scr, *, tq, tk):
    h = pl.program_id(1)
    s = q_ref.shape[0]
    k_scr[...] = k_ref[...].astype(BF16)
    v_scr[...] = v_ref[...].astype(BF16)
    lane = lax.broadcasted_iota(jnp.int32, (s, LANES), 1)
    cq = jnp.sum(jnp.where(lane == h, c_ref[...], 0.0), axis=-1, keepdims=True)
    ck_row = ct_ref[pl.ds(h, 1), :]
    for j in range(s // tk):
        ck_scr[j] = jnp.broadcast_to(ck_row[:, j * tk:(j + 1) * tk], (SUBLANES, tk))
    for qi in range(s // tq):
        rows = slice(qi * tq, (qi + 1) * tq)
        n_kv = (qi * tq + tq + tk - 1) // tk
        o = _flash_rows(q_ref[rows, :].astype(BF16), k_scr, v_scr, n_kv, qi * tq, tk, (cq[rows], ck_scr))
        o_ref[rows, :] = o.astype(o_ref.dtype)


def fox_prompt_attention(qf, kf, vf, c, ct, batch, seq, heads, tq=256, tk=256):
    blk = pl.BlockSpec((seq, HEAD_DIM), lambda b, h: (b, h))
    return pl.pallas_call(
        functools.partial(_fox_prompt_kernel, tq=tq, tk=tk),
        grid=(batch, heads),
        in_specs=[blk, blk, blk,
                  pl.BlockSpec((seq, LANES), lambda b, h: (b, 0)),
                  pl.BlockSpec((LANES, seq), lambda b, h: (b, 0))],
        out_specs=blk,
        out_shape=jax.ShapeDtypeStruct((batch * seq, heads * HEAD_DIM), BF16),
        scratch_shapes=[pltpu.VMEM((seq, HEAD_DIM), BF16), pltpu.VMEM((seq, HEAD_DIM), BF16),
                        pltpu.VMEM((seq // tk, SUBLANES, tk), F32)],
        compiler_params=_params(("parallel", "parallel")),
        name="fox_prompt_attention",
    )(qf, kf, vf, c, ct)


def _lambda_value(lq1, lk1, lq2, lk2, lam_init):
    a = jnp.sum(lq1[...] * lk1[...], axis=-1, keepdims=True)
    b = jnp.sum(lq2[...] * lk2[...], axis=-1, keepdims=True)
    return jnp.exp(a) - jnp.exp(b) + lam_init


def _sub_layer_norm(od, g, lam_init):
    ms = jnp.mean(od * od, axis=-1, keepdims=True)
    return (od * lax.rsqrt(ms + SUBLN_EPS) * g) * (1.0 - lam_init)


def _diff_prompt_kernel(q_ref, k_ref, v_ref, lq1, lk1, lq2, lk2, g_ref, o_ref, k_scr, v_scr, *, tq, tk, lam_init):
    s = q_ref.shape[0]
    lam = _lambda_value(lq1, lk1, lq2, lk2, lam_init)
    k_scr[0] = k_ref[:, :HEAD_DIM].astype(BF16)
    k_scr[1] = k_ref[:, HEAD_DIM:].astype(BF16)
    v_scr[...] = v_ref[...].astype(BF16)
    for qi in range(s // tq):
        rows = slice(qi * tq, (qi + 1) * tq)
        n_kv = (qi * tq + tq + tk - 1) // tk
        o1 = _flash_rows(q_ref[rows, :HEAD_DIM].astype(BF16), k_scr.at[0], v_scr, n_kv, qi * tq, tk, None)
        o2 = _flash_rows(q_ref[rows, HEAD_DIM:].astype(BF16), k_scr.at[1], v_scr, n_kv, qi * tq, tk, None)
        o_ref[rows, :] = _sub_layer_norm(o1 - lam * o2, g_ref[...], lam_init).astype(o_ref.dtype)


def diff_prompt_attention(qd, kd, vd, lams, g_sub, lam_init, batch, seq, heads, tq=256, tk=256):
    blk = pl.BlockSpec((seq, 2 * HEAD_DIM), lambda b, h: (b, h))
    vec = pl.BlockSpec((1, HEAD_DIM), lambda b, h: (0, 0))
    return pl.pallas_call(
        functools.partial(_diff_prompt_kernel, tq=tq, tk=tk, lam_init=lam_init),
        grid=(batch, heads),
        in_specs=[blk, blk, blk, vec, vec, vec, vec,
                  pl.BlockSpec((1, 2 * HEAD_DIM), lambda b, h: (0, 0))],
        out_specs=blk,
        out_shape=jax.ShapeDtypeStruct((batch * seq, heads * 2 * HEAD_DIM), BF16),
        scratch_shapes=[pltpu.VMEM((2, seq, HEAD_DIM), BF16), pltpu.VMEM((seq, 2 * HEAD_DIM), BF16)],
        compiler_params=_params(("parallel", "parallel")),
        name="diff_prompt_attention",
    )(qd, kd, vd, *lams, g_sub)


def _online_update(s, mask, v_bf, m_scr, l_scr, acc_scr):
    s = jnp.where(mask, s, NEG_INF)
    m_prev = m_scr[...]
    m_new = jnp.maximum(m_prev, jnp.max(s, axis=-1, keepdims=True))
    alpha = jnp.exp(m_prev - m_new)
    p = jnp.exp(s - m_new)
    l_scr[...] = alpha * l_scr[...] + jnp.sum(p, axis=-1, keepdims=True)
    acc_scr[...] = alpha * acc_scr[...] + jnp.dot(p.astype(BF16), v_bf, preferred_element_type=F32)
    m_scr[...] = m_new


def _split3(x, lane):
    hi = x.astype(BF16).astype(F32)
    r1 = x - hi
    mid = r1.astype(BF16).astype(F32)
    lo = r1 - mid
    n = LANES // SUBLANES
    return jnp.where(lane < n, hi, jnp.where(lane < 2 * n, mid, jnp.where(lane < 3 * n, lo, 0.0)))


def _rows_by_piece(x, width, order):
    return jnp.concatenate([x[:, j * width:(j + 1) * width] for j in order], axis=0)


def _fox_decode_kernel(pt_ref, q_ref, kn_ref, vn_ref, lfn_ref, *rest, heads, group, tnew):
    del pt_ref
    page_refs, o_ref = rest[:3 * group], rest[3 * group]
    q_scr, m_scr, l_scr, acc_scr, later_scr = rest[3 * group + 1:]
    j = pl.program_id(1)
    nrow = heads * tnew
    row = lax.broadcasted_iota(jnp.int32, (nrow, LANES), 0)
    lane = lax.broadcasted_iota(jnp.int32, (nrow, LANES), 1)
    order = range(heads)

    @pl.when(j == 0)
    def _():
        for g in range(group):
            tok = slice(g * tnew, (g + 1) * tnew)
            qm = _rows_by_piece(q_ref[tok, :], HEAD_DIM, order) * ATTN_SCALE
            onehot = jnp.where((lane < LOGF_COPIES * heads) & ((lane & (heads - 1)) == (row >> 3)), 1.0, 0.0)
            q_scr[g] = jnp.concatenate([qm, onehot], axis=1).astype(BF16)
            m_scr[g] = jnp.full(m_scr.shape[1:], NEG_INF, F32)
            l_scr[g] = jnp.zeros(l_scr.shape[1:], F32)
            acc_scr[g] = jnp.zeros(acc_scr.shape[1:], F32)
            later_scr[g] = jnp.zeros(later_scr.shape[1:], F32)
            lfn = jnp.concatenate([lfn_ref[tok, :], jnp.zeros((LANES - tnew, LANES), F32)], axis=0)
            tri = jnp.where(row >= lane, 1.0, 0.0).astype(F32)
            cn = jnp.dot(tri, lfn, preferred_element_type=F32, precision=lax.Precision.HIGHEST)[0:tnew]
            bn = _split3(-cn, lane[0:tnew])
            kn = _rows_by_piece(kn_ref[tok, :], HEAD_DIM, order)
            kaug = jnp.concatenate([kn, jnp.concatenate([bn] * heads, axis=0)], axis=1).astype(BF16)
            s = lax.dot_general(q_scr[g], kaug, _NT, preferred_element_type=F32)
            mask = ((lane >> 3) == (row >> 3)) & ((lane & 7) <= (row & 7))
            _online_update(s, mask, _rows_by_piece(vn_ref[tok, :], HEAD_DIM, order).astype(BF16),
                           m_scr.at[g], l_scr.at[g], acc_scr.at[g])

    for g in range(group):
        kp_ref, vp_ref, lfp_ref = page_refs[3 * g:3 * g + 3]
        lf = lfp_ref[0]
        triu = jnp.where(row < lane, 1.0, 0.0).astype(F32)
        suf = jnp.dot(triu, lf, preferred_element_type=F32, precision=lax.Precision.HIGHEST) + later_scr[g]
        later_scr[g] = later_scr[g] + jnp.sum(lf, axis=0, keepdims=True)
        bk = _split3(suf, lane)
        page = bk.shape[0]
        bk_rows = jnp.broadcast_to(bk[:, None, :], (page, heads, LANES)).reshape(page * heads, LANES)
        kaug = jnp.concatenate([kp_ref[0].astype(BF16), bk_rows.astype(BF16)], axis=1)
        s = lax.dot_general(q_scr[g], kaug, _NT, preferred_element_type=F32)
        col = lax.broadcasted_iota(jnp.int32, s.shape, 1)
        rw = lax.broadcasted_iota(jnp.int32, s.shape, 0)
        _online_update(s, (col & (heads - 1)) == (rw >> 3), vp_ref[0].astype(BF16),
                       m_scr.at[g], l_scr.at[g], acc_scr.at[g])

    @pl.when(j == pl.num_programs(1) - 1)
    def _():
        for g in range(group):
            o = acc_scr[g] / l_scr[g]
            o_ref[g * tnew:(g + 1) * tnew, :] = jnp.concatenate(
                [o[h * tnew:(h + 1) * tnew] for h in range(heads)], axis=1)


def fox_decode_attention(page_table, qf, kf, vf, lfn, cache_k, cache_v, cache_lf3, dec_batch, tnew, heads):
    n_pages = page_table.shape[1]
    width = heads * HEAD_DIM
    rows_kv = cache_k.shape[1]
    page = cache_lf3.shape[1]
    group = DECODE_GROUP if dec_batch % DECODE_GROUP == 0 else 1
    tok = pl.BlockSpec((group * tnew, width), lambda b, j, pt: (b, 0))

    def pg(g):
        return lambda b, j, pt: (pt[(b * group + g) * n_pages + n_pages - 1 - j], 0, 0)

    page_specs = []
    for g in range(group):
        page_specs += [pl.BlockSpec((1, rows_kv, HEAD_DIM), pg(g)), pl.BlockSpec((1, rows_kv, HEAD_DIM), pg(g)),
                       pl.BlockSpec((1, page, LANES), pg(g))]
    nrow = heads * tnew
    grid_spec = pltpu.PrefetchScalarGridSpec(
        num_scalar_prefetch=1,
        grid=(dec_batch // group, n_pages),
        in_specs=[tok, tok, tok, pl.BlockSpec((group * tnew, LANES), lambda b, j, pt: (b, 0))] + page_specs,
        out_specs=tok,
        scratch_shapes=[pltpu.VMEM((group, nrow, 2 * HEAD_DIM), BF16),
                        pltpu.VMEM((group, nrow, 1), F32), pltpu.VMEM((group, nrow, 1), F32),
                        pltpu.VMEM((group, nrow, HEAD_DIM), F32), pltpu.VMEM((group, 1, LANES), F32)])
    return pl.pallas_call(
        functools.partial(_fox_decode_kernel, heads=heads, group=group, tnew=tnew),
        grid_spec=grid_spec,
        out_shape=jax.ShapeDtypeStruct((dec_batch * tnew, width), F32),
        compiler_params=_params(("parallel", "arbitrary")),
        name="fox_decode_attention",
    )(page_table.reshape(-1), qf, kf, vf, lfn, *([cache_k, cache_v, cache_lf3] * group))


def _diff_decode_kernel(pt_ref, q_ref, kn_ref, vn_ref, lq1, lk1, lq2, lk2, g_ref, *rest, heads, lam_init, group, tnew):
    del pt_ref
    page_refs, o_ref = rest[:2 * group], rest[2 * group]
    q_scr, m_scr, l_scr, acc_scr = rest[2 * group + 1:]
    j = pl.program_id(1)
    half = heads * tnew
    nrow = 2 * half
    order = [2 * h for h in range(heads)] + [2 * h + 1 for h in range(heads)]

    @pl.when(j == 0)
    def _():
        row = lax.broadcasted_iota(jnp.int32, (nrow, nrow), 0)
        lane = lax.broadcasted_iota(jnp.int32, (nrow, nrow), 1)
        for g in range(group):
            tok = slice(g * tnew, (g + 1) * tnew)
            q_scr[g] = (_rows_by_piece(q_ref[tok, :], HEAD_DIM, order) * ATTN_SCALE).astype(BF16)
            m_scr[g] = jnp.full(m_scr.shape[1:], NEG_INF, F32)
            l_scr[g] = jnp.zeros(l_scr.shape[1:], F32)
            acc_scr[g] = jnp.zeros(acc_scr.shape[1:], F32)
            kn = _rows_by_piece(kn_ref[tok, :], HEAD_DIM, order).astype(BF16)
            s = lax.dot_general(q_scr[g], kn, _NT, preferred_element_type=F32)
            mask = ((lane >> 3) == (row >> 3)) & ((lane & 7) <= (row & 7))
            vn = _rows_by_piece(vn_ref[tok, :], 2 * HEAD_DIM, [jj // 2 for jj in order]).astype(BF16)
            _online_update(s, mask, vn, m_scr.at[g], l_scr.at[g], acc_scr.at[g])

    for g in range(group):
        kp_ref, vp_ref = page_refs[2 * g:2 * g + 2]
        n_keys = vp_ref.shape[1]
        s = jnp.concatenate(
            [lax.dot_general(q_scr[g, c * half:(c + 1) * half, :],
                             kp_ref[0, pl.ds(c, n_keys, stride=2), :].astype(BF16), _NT, preferred_element_type=F32)
             for c in range(2)], axis=0)
        col = lax.broadcasted_iota(jnp.int32, s.shape, 1)
        rw = lax.broadcasted_iota(jnp.int32, s.shape, 0)
        _online_update(s, (col & (heads - 1)) == ((rw & (half - 1)) >> 3), vp_ref[0].astype(BF16),
                       m_scr.at[g], l_scr.at[g], acc_scr.at[g])

    @pl.when(j == pl.num_programs(1) - 1)
    def _():
        lam = _lambda_value(lq1, lk1, lq2, lk2, lam_init)
        for g in range(group):
            o = acc_scr[g] / l_scr[g]
            y = _sub_layer_norm(o[:half] - lam * o[half:], g_ref[...], lam_init)
            o_ref[g * tnew:(g + 1) * tnew, :] = jnp.concatenate(
                [y[h * tnew:(h + 1) * tnew] for h in range(heads)], axis=1)


def diff_decode_attention(page_table, qd, kd, vd, lams, g_sub, lam_init, cache_k, cache_v, dec_batch, tnew, heads):
    n_pages = page_table.shape[1]
    width = heads * 2 * HEAD_DIM
    group = DECODE_GROUP if dec_batch % DECODE_GROUP == 0 else 1
    tok = pl.BlockSpec((group * tnew, width), lambda b, j, pt: (b, 0))
    vec = pl.BlockSpec((1, HEAD_DIM), lambda b, j, pt: (0, 0))

    def pg(g):
        return lambda b, j, pt: (pt[(b * group + g) * n_pages + j], 0, 0)

    page_specs = []
    for g in range(group):
        page_specs += [pl.BlockSpec((1,) + cache_k.shape[1:], pg(g)), pl.BlockSpec((1,) + cache_v.shape[1:], pg(g))]
    nrow = 2 * heads * tnew
    grid_spec = pltpu.PrefetchScalarGridSpec(
        num_scalar_prefetch=1,
        grid=(dec_batch // group, n_pages),
        in_specs=[tok, tok, tok, vec, vec, vec, vec,
                  pl.BlockSpec((1, 2 * HEAD_DIM), lambda b, j, pt: (0, 0))] + page_specs,
        out_specs=tok,
        scratch_shapes=[pltpu.VMEM((group, nrow, HEAD_DIM), BF16),
                        pltpu.VMEM((group, nrow, 1), F32), pltpu.VMEM((group, nrow, 1), F32),
                        pltpu.VMEM((group, nrow, 2 * HEAD_DIM), F32)])
    return pl.pallas_call(
        functools.partial(_diff_decode_kernel, heads=heads, lam_init=lam_init, group=group, tnew=tnew),
        grid_spec=grid_spec,
        out_shape=jax.ShapeDtypeStruct((dec_batch * tnew, width), F32),
        compiler_params=_params(("parallel", "arbitrary")),
        name="diff_decode_attention",
    )(page_table.reshape(-1), qd, kd, vd, *lams, g_sub, *([cache_k, cache_v] * group))


def _topk_rows(s, payload=None):
    n, t = s.shape
    rid = lax.broadcasted_iota(jnp.int32, (n, t), 0).astype(F32)
    slot = lax.broadcasted_iota(jnp.int32, (PEER_TOPK, t), 0)
    vals = jnp.zeros((PEER_TOPK, t), F32)
    idxs = jnp.zeros((PEER_TOPK, t), F32)
    pays = jnp.zeros((PEER_TOPK, t), F32)
    for r in range(PEER_TOPK):
        m = jnp.max(s, axis=0, keepdims=True)
        idx = jnp.min(jnp.where(s == m, rid, float(n)), axis=0, keepdims=True)
        hit = rid == idx
        vals = jnp.where(slot == r, m, vals)
        idxs = jnp.where(slot == r, idx, idxs)
        if payload is not None:
            pays = jnp.where(slot == r, jnp.sum(jnp.where(hit, payload, 0.0), axis=0, keepdims=True), pays)
        s = jnp.where(hit, -jnp.inf, s)
    return vals, idxs, pays


def _router_kernel(h_ref, wq_ref, ka_ref, kb_ref, g_ref, e_ref):
    t = h_ref.shape[0]
    q = jnp.dot(h_ref[...].astype(BF16), wq_ref[...], preferred_element_type=F32)
    ka = ka_ref[...].astype(BF16)
    kb = kb_ref[...].astype(BF16)
    half = ka.shape[1]
    for h in range(PEER_HEADS):
        qa = q[:, (2 * h) * half:(2 * h + 1) * half].astype(BF16)
        qb = q[:, (2 * h + 1) * half:(2 * h + 2) * half].astype(BF16)
        sa = lax.dot_general(ka, qa, _NT, preferred_element_type=F32)
        sb = lax.dot_general(kb, qb, _NT, preferred_element_type=F32)
        va, ia, _ = _topk_rows(sa)
        vb, ib, _ = _topk_rows(sb)
        rep = lambda x: jnp.concatenate(
            [jnp.broadcast_to(x[i:i + 1, :], (PEER_TOPK, t)) for i in range(PEER_TOPK)], axis=0)
        tile = lambda x: jnp.concatenate([x] * PEER_TOPK, axis=0)
        cand = rep(va) + tile(vb)
        cidx = rep(ia) * float(N_KEYS) + tile(ib)
        sv, _, e = _topk_rows(cand, cidx)
        ex = jnp.exp(sv - sv[0:1, :])
        g_ref[h * PEER_TOPK:(h + 1) * PEER_TOPK, :] = ex / jnp.sum(ex, axis=0, keepdims=True)
        e_ref[h * PEER_TOPK:(h + 1) * PEER_TOPK, :] = e.astype(jnp.int32)


def peer_router(h, wq, keys_a, keys_b, tm=256):
    n, d = h.shape
    tm = min(tm, n)
    return pl.pallas_call(
        _router_kernel,
        grid=(n // tm,),
        in_specs=[pl.BlockSpec((tm, d), lambda i: (i, 0)),
                  pl.BlockSpec(wq.shape, lambda i: (0, 0)),
                  pl.BlockSpec(keys_a.shape, lambda i: (0, 0)),
                  pl.BlockSpec(keys_b.shape, lambda i: (0, 0))],
        out_specs=[pl.BlockSpec((PEER_PAIRS, tm), lambda i: (0, i)),
                   pl.BlockSpec((PEER_PAIRS, tm), lambda i: (0, i))],
        out_shape=[jax.ShapeDtypeStruct((PEER_PAIRS, n), F32),
                   jax.ShapeDtypeStruct((PEER_PAIRS, n), jnp.int32)],
        compiler_params=_params(("parallel",)),
        name="peer_router",
    )(h, wq, keys_a, keys_b)


def pack_bf16_halves(t):
    half = t.shape[1] // 2
    b = lax.bitcast_convert_type(t.astype(BF16), jnp.uint16).astype(jnp.uint32)
    return lax.bitcast_convert_type((b[:, half:] << 16) | b[:, :half], jnp.int32)


def _unpack_bf16_halves(words):
    lo = lax.bitcast_convert_type(words << 16, F32)
    hi = lax.bitcast_convert_type(words & jnp.int32(-65536), F32)
    return lo, hi


PEER_SLOTS = 4
PEER_AHEAD = 2


def _peer_expert_kernel(idx_ref, h_ref, x_ref, g_ref, uv_hbm, o_ref, *scratch):
    bufs = scratch[:PEER_SLOTS]
    sem, xb_scr, part_scr = scratch[PEER_SLOTS:]
    tb, d = h_ref.shape
    half = d // 2
    n_chunks = half // LANES
    groups = PEER_PAIRS // SUBLANES
    per_chunk = PEER_PAIRS // (2 * n_chunks)

    def start_rows(t, slot, ks):
        for k in ks:
            e = idx_ref[t, k]
            pltpu.make_async_copy(uv_hbm.at[pl.ds(e, 1), :], bufs[slot].at[pl.ds(k, 1), :], sem.at[slot]).start()

    def wait(slot):
        pltpu.make_async_copy(uv_hbm.at[pl.ds(0, PEER_PAIRS), :], bufs[slot], sem.at[slot]).wait()

    lane = lax.broadcasted_iota(jnp.int32, (PEER_PAIRS, tb), 1)
    for a in range(PEER_AHEAD):
        start_rows(a, a, range(PEER_PAIRS))

    def token(t, slot):
        t_next = jnp.minimum(t + PEER_AHEAD, tb - 1)
        s_next = (slot + PEER_AHEAD) % PEER_SLOTS
        wait(slot)
        xb_scr[...] = jnp.broadcast_to(h_ref[pl.ds(t, 1), :], (SUBLANES, d))
        acc = jnp.zeros((groups, SUBLANES, LANES), F32)
        for c in range(n_chunks):
            start_rows(t_next, s_next, range(c * per_chunk, (c + 1) * per_chunk))
            sl = slice(c * LANES, (c + 1) * LANES)
            sh = slice(half + c * LANES, half + (c + 1) * LANES)
            lo, hi = _unpack_bf16_halves(bufs[slot][:, sl].reshape(groups, SUBLANES, LANES))
            acc = acc + lo * xb_scr[:, sl][None] + hi * xb_scr[:, sh][None]
        act = jnp.sum(acc.reshape(PEER_PAIRS, LANES), axis=-1, keepdims=True)
        gate = jnp.sum(jnp.where(lane == t, g_ref[...], 0.0), axis=-1, keepdims=True)
        w = gate * (0.5 * act * (1.0 + lax.erf(act * math.sqrt(0.5))))
        wb = jnp.broadcast_to(w, (PEER_PAIRS, LANES)).reshape(groups, SUBLANES, LANES)
        for c in range(n_chunks):
            start_rows(t_next, s_next, range((n_chunks + c) * per_chunk, (n_chunks + c + 1) * per_chunk))
            sl = slice(c * LANES, (c + 1) * LANES)
            sh = slice(half + c * LANES, half + (c + 1) * LANES)
            lo, hi = _unpack_bf16_halves(bufs[slot][:, sh].reshape(groups, SUBLANES, LANES))
            part_scr[:, sl] = jnp.sum(lo * wb, axis=0)
            part_scr[:, sh] = jnp.sum(hi * wb, axis=0)
        o_ref[pl.ds(t, 1), :] = x_ref[pl.ds(t, 1), :] + jnp.sum(part_scr[...], axis=0, keepdims=True)

    def body(i, _):
        for slot in range(PEER_SLOTS):
            token(i * PEER_SLOTS + slot, slot)
        return 0

    lax.fori_loop(0, tb // PEER_SLOTS, body, 0)
    for a in range(PEER_AHEAD):
        wait((tb + a) % PEER_SLOTS)


def peer_experts(eidx, h, x, gates, uv_packed, tb=128):
    n, d = h.shape
    tb = min(tb, n)
    assert tb % PEER_SLOTS == 0 and n % tb == 0 and PEER_AHEAD < PEER_SLOTS
    tok = pl.BlockSpec((tb, d), lambda i: (i, 0))
    buf = pltpu.VMEM((PEER_PAIRS, d), jnp.int32)
    return pl.pallas_call(
        _peer_expert_kernel,
        grid=(n // tb,),
        in_specs=[pl.BlockSpec((tb, PEER_PAIRS), lambda i: (i, 0), memory_space=pltpu.SMEM),
                  tok, tok,
                  pl.BlockSpec((PEER_PAIRS, tb), lambda i: (0, i)),
                  pl.BlockSpec(memory_space=pl.ANY)],
        out_specs=tok,
        out_shape=jax.ShapeDtypeStruct((n, d), F32),
        scratch_shapes=[buf] * PEER_SLOTS + [
            pltpu.SemaphoreType.DMA((PEER_SLOTS,)),
            pltpu.VMEM((SUBLANES, d), F32), pltpu.VMEM((SUBLANES, d), F32)],
        compiler_params=_params(("arbitrary",)),
        name="peer_experts",
    )(eidx, h, x, gates, uv_packed)


def _rope_tables(pos):
    inv = ROPE_THETA ** (-jnp.arange(0, HEAD_DIM, 2, dtype=F32) / HEAD_DIM)
    ang = pos.astype(F32)[:, None] * inv[None, :]
    cos, sin = jnp.cos(ang), jnp.sin(ang)
    return jnp.concatenate([cos, cos], axis=-1), jnp.concatenate([-sin, sin], axis=-1)


def _mixer_inputs(x, pos, w, tm):
    h = rmsnorm(x, w["g_attn"], BF16, tm=min(tm, 256))
    cos, sin = _rope_tables(pos)
    hn = functools.partial(proj, [h], mode="headnorm", aux_kinds=("col",), tm=tm)
    hr = functools.partial(proj, [h], mode="headnorm_rope", aux_kinds=("col", "row", "row"), tm=tm)
    qf = hn([w["w_fq"]], aux=(w["g_qf"],))
    kf = hn([w["w_fk"]], aux=(w["g_kf"],))
    vf = proj([h], [w["w_fv"]], "plain", tm=tm)
    lf3 = proj([h], [w["w_ff3"]], "logsig", aux=(w["b_ff3"],), aux_kinds=("bias",), tm=tm)
    qd = hr([w["w_dq"]], aux=(w["g_qd"], cos, sin))
    kd = hr([w["w_dk"]], aux=(w["g_kd"], cos, sin))
    vd = proj([h], [w["w_dv"]], "plain", tm=tm)
    return qf, kf, vf, lf3, qd, kd, vd


def _merge_and_ffn(x, of, od, w, tm):
    x1 = proj([of, od], [w["w_out_f"], w["w_out_d"]], "residual", aux=(x,), aux_kinds=("full",), tm=tm)
    h2 = rmsnorm(x1, w["g_ffn"], F32, tm=min(tm, 256))
    gates, eidx_t = peer_router(h2, w["w_peer_q"], w["keys_a"], w["keys_b"])
    return peer_experts(eidx_t.T, h2, x1, gates, w["peer_uv"])


def kernel(x_prompt, x_sample, cache_fox_k, cache_fox_v, cache_fox_logf, cache_diff_k, cache_diff_v, page_table, g_attn_norm, w_in, b_forget, g_q_fox, g_k_fox, g_q_diff, g_k_diff, lambda_q1, lambda_k1, lambda_q2, lambda_k2, g_diff_sub, w_out, g_ffn_norm, w_peer_q, peer_keys_a, peer_keys_b, peer_u, peer_v):
    batch, seq, d_model = x_prompt.shape
    dec_batch, dec_seq, _ = x_sample.shape
    depth, n_pool, page_size, fox_heads, _ = cache_fox_k.shape
    diff_heads = cache_diff_k.shape[3]
    fox_w = fox_heads * HEAD_DIM
    diff_w = diff_heads * 2 * HEAD_DIM
    past_len = page_table.shape[1] * page_size
    assert fox_heads * LOGF_COPIES <= LANES and dec_seq == SUBLANES

    xp = x_prompt.reshape(batch * seq, d_model)
    xs = x_sample.reshape(dec_batch * dec_seq, d_model)
    pos_p = jnp.tile(jnp.arange(seq, dtype=jnp.int32), batch)
    pos_s = jnp.tile(past_len + jnp.arange(dec_seq, dtype=jnp.int32), dec_batch)
    outs_p, outs_s = [], []
    for l in range(depth):
        lam_init = 0.8 - 0.6 * math.exp(-0.3 * l)
        wi = w_in[l]
        o = 0
        sec = {}
        for name, width in (("w_fq", fox_w), ("w_fk", fox_w), ("w_fv", fox_w), ("w_ff", fox_heads),
                            ("w_dq", diff_w), ("w_dk", diff_w), ("w_dv", diff_w)):
            sec[name] = wi[:, o:o + width]
            o += width
        w_ff = sec.pop("w_ff")
        pad = LANES - LOGF_COPIES * fox_heads
        w = {k: v.astype(BF16) for k, v in sec.items()}
        w["w_ff3"] = jnp.pad(jnp.tile(w_ff, (1, LOGF_COPIES)), ((0, 0), (0, pad))).astype(BF16)
        w["b_ff3"] = jnp.pad(jnp.tile(b_forget[l], LOGF_COPIES), (0, pad)).reshape(1, LANES)
        w["g_attn"], w["g_ffn"] = g_attn_norm[l], g_ffn_norm[l]
        for name, g in (("g_qf", g_q_fox), ("g_kf", g_k_fox), ("g_qd", g_q_diff), ("g_kd", g_k_diff)):
            w[name] = g[l].reshape(1, HEAD_DIM)
        w["w_out_f"] = w_out[l, :fox_w].astype(BF16)
        w["w_out_d"] = w_out[l, fox_w:].astype(BF16)
        w["w_peer_q"] = w_peer_q[l].astype(BF16)
        w["keys_a"], w["keys_b"] = peer_keys_a[l], peer_keys_b[l]
        w["peer_uv"] = jnp.concatenate([pack_bf16_halves(peer_u[l]), pack_bf16_halves(peer_v[l])], axis=1)
        lams = tuple(a[l].reshape(1, HEAD_DIM) for a in (lambda_q1, lambda_k1, lambda_q2, lambda_k2))
        g_sub = g_diff_sub[l].reshape(1, 2 * HEAD_DIM)

        qf, kf, vf, lf3, qd, kd, vd = _mixer_inputs(xp, pos_p, w, tm=512)
        c, ct = cumsum_logf(lf3, batch, seq)
        of = fox_prompt_attention(qf, kf, vf, c, ct, batch, seq, fox_heads)
        od = diff_prompt_attention(qd, kd, vd, lams, g_sub, lam_init, batch, seq, diff_heads)
        xp = _merge_and_ffn(xp, of, od, w, tm=512)
        outs_p.append((kf, vf, lf3[:, :fox_heads], kd, vd))

        qf, kf, vf, lf3, qd, kd, vd = _mixer_inputs(xs, pos_s, w, tm=256)
        lf_cache = cache_fox_logf[l].astype(F32)
        lf_cache3 = jnp.pad(jnp.tile(lf_cache, (1, 1, LOGF_COPIES)), ((0, 0), (0, 0), (0, pad)))
        of = fox_decode_attention(page_table, qf, kf, vf, lf3,
                                  cache_fox_k[l].reshape(n_pool, page_size * fox_heads, HEAD_DIM),
                                  cache_fox_v[l].reshape(n_pool, page_size * fox_heads, HEAD_DIM),
                                  lf_cache3, dec_batch, dec_seq, fox_heads)
        od = diff_decode_attention(page_table, qd, kd, vd, lams, g_sub, lam_init,
                                   cache_diff_k[l].reshape(n_pool, page_size * diff_heads * 2, HEAD_DIM),
                                   cache_diff_v[l].reshape(n_pool, page_size * diff_heads, 2 * HEAD_DIM),
                                   dec_batch, dec_seq, diff_heads)
        xs = _merge_and_ffn(xs, of.astype(BF16), od.astype(BF16), w, tm=256)
        outs_s.append((kf, vf, lf3[:, :fox_heads], kd, vd))

    def stack(outs, i, shape):
        return jnp.stack([o[i].reshape(shape) for o in outs])

    res = [xp.reshape(batch, seq, d_model), xs.reshape(dec_batch, dec_seq, d_model)]
    for outs, b, t in ((outs_p, batch, seq), (outs_s, dec_batch, dec_seq)):
        res += [stack(outs, 0, (b, t, fox_heads, HEAD_DIM)), stack(outs, 1, (b, t, fox_heads, HEAD_DIM)),
                stack(outs, 2, (b, t, fox_heads)), stack(outs, 3, (b, t, diff_heads, 2, HEAD_DIM)),
                stack(outs, 4, (b, t, diff_heads, 2 * HEAD_DIM))]
    return tuple(res)
```
